```python
import math
import jax, jax.numpy as jnp
from jax import lax
import numpy as np

D_MODEL = 2048
BATCH = 2
SEQ = 4096
DEPTH = 1
DEC_BATCH = 32
DEC_SEQ = 8
PAST_LEN = 8192
PAGE_SIZE = 128

N_RET_HEADS = 8
RET_DK = 128
RET_DV = 128
RET_CHUNK = 128
ROPE_BASE = 10000.0
N_DIFF_HEADS = 8
DIFF_D = 64
DIFF_DV = 2 * DIFF_D
Q_BLOCK = 128
N_BUCKETS = 32
MAX_EXACT = N_BUCKETS // 2
MAX_DISTANCE = 128
N_MEM = 256
N_XHEADS = 4
X_HEAD_DIM = 128
D_FF = -(-8 * D_MODEL // (3 * 256)) * 256
RMS_EPS = 1e-6
RET_W = N_RET_HEADS * RET_DV
DIFF_W = N_DIFF_HEADS * DIFF_DV
MIX_WIDTH = RET_W + DIFF_W
IN_COLS = (N_RET_HEADS * RET_DK, N_RET_HEADS * RET_DK, RET_W, RET_W,
           N_DIFF_HEADS * 2 * DIFF_D, N_DIFF_HEADS * 2 * DIFF_D, DIFF_W)
D_IN = sum(IN_COLS)
IN_SPLITS = tuple(int(v) for v in np.cumsum(IN_COLS)[:-1])

kernel_name = 'hymba_retention_diffattn_decoder_step'


def rms_norm(x, g):
    xf = x.astype(jnp.float32)
    y = xf * lax.rsqrt(jnp.mean(xf * xf, axis=-1, keepdims=True) + RMS_EPS)
    return (y * g.astype(jnp.float32)).astype(x.dtype)


def head_rms_norm(x, g):
    h, d = x.shape[-2], x.shape[-1]
    xf = x.astype(jnp.float32)
    y = xf * lax.rsqrt(jnp.mean(xf * xf, axis=-1, keepdims=True) + RMS_EPS)
    return y * g.astype(jnp.float32).reshape(h, d)


def rotate(x, pos):
    d = x.shape[-1]
    inv_freq = ROPE_BASE ** (-jnp.arange(0, d, 2, dtype=jnp.float32) / d)
    ang = pos.astype(jnp.float32)[:, None] * inv_freq[None, :]
    cos = jnp.cos(ang)[None, :, None, :]
    sin = jnp.sin(ang)[None, :, None, :]
    xf = x.astype(jnp.float32)
    x1, x2 = xf[..., 0::2], xf[..., 1::2]
    return jnp.stack([x1 * cos - x2 * sin, x1 * sin + x2 * cos], axis=-1).reshape(x.shape)


def retention_chunked(q, k, v, s0):
    b, t, h, dk = q.shape
    dv = v.shape[-1]
    c = math.gcd(t, RET_CHUNK)
    n = t // c
    log_g = jnp.log1p(-(2.0 ** (-5.0 - jnp.arange(h, dtype=jnp.float32))))
    idx = jnp.arange(c, dtype=jnp.float32)
    rel = idx[:, None] - idx[None, :]
    intra = jnp.where(rel[None] >= 0, jnp.exp(log_g[:, None, None] * jnp.maximum(rel, 0.0)[None]), 0.0)
    q_dec = jnp.exp(log_g[None, :] * (idx[:, None] + 1.0))[None, :, :, None]
    k_dec = jnp.exp(log_g[None, :] * (c - 1.0 - idx[:, None]))[None, :, :, None]
    chunk_dec = jnp.exp(log_g * c)[None, :, None, None]

    def to_chunks(a):
        return jnp.moveaxis(a.astype(jnp.float32).reshape(b, n, c, h, a.shape[-1]), 1, 0)

    def step(s, inp):
        qb, kb, vb = inp
        scores = jnp.einsum('bihd,bjhd->bhij', qb, kb) * intra[None]
        o = jnp.einsum('bhij,bjhv->bihv', scores, vb) + jnp.einsum('bihd,bhdv->bihv', qb, s) * q_dec
        s_new = s * chunk_dec + jnp.einsum('bjhd,bjhv->bhdv', kb * k_dec, vb)
        return s_new, o

    s_fin, o = lax.scan(step, s0.astype(jnp.float32), (to_chunks(q), to_chunks(k), to_chunks(v)))
    return jnp.moveaxis(o, 0, 1).reshape(b, t, h, dv), s_fin


def t5_bucket(qpos, kpos):
    n = jnp.maximum(qpos[:, None] - kpos[None, :], 0)
    nf = jnp.maximum(n, 1).astype(jnp.float32)
    large = MAX_EXACT + (jnp.log(nf / MAX_EXACT) / math.log(MAX_DISTANCE / MAX_EXACT)
                         * (N_BUCKETS - MAX_EXACT)).astype(jnp.int32)
    large = jnp.minimum(large, N_BUCKETS - 1)
    return jnp.where(n < MAX_EXACT, n, large)


def diff_attend(q, k, v, qpos, kpos, lam, rel_bias):
    scale = DIFF_D ** -0.5
    bias = jnp.transpose(rel_bias.astype(jnp.float32)[t5_bucket(qpos, kpos)], (2, 0, 1))[None]
    causal = (kpos[None, :] <= qpos[:, None])[None, None]

    def softmax_map(qa, ka):
        s = jnp.einsum('bqhd,bkhd->bhqk', qa, ka, preferred_element_type=jnp.float32) * scale + bias
        return jax.nn.softmax(jnp.where(causal, s, -1e30), axis=-1)

    a = softmax_map(q[..., :DIFF_D], k[..., :DIFF_D]) - lam * softmax_map(q[..., DIFF_D:], k[..., DIFF_D:])
    return jnp.einsum('bhqk,bkhv->bqhv', a, v.astype(jnp.float32))


def cross_attend(h, mem_k, mem_v, w_xq, w_xo):
    b, t, _ = h.shape
    q = (h @ w_xq).reshape(b, t, N_XHEADS, X_HEAD_DIM)
    s = jnp.einsum('bqhd,bkhd->bhqk', q, mem_k, preferred_element_type=jnp.float32) * X_HEAD_DIM ** -0.5
    p = jax.nn.softmax(s, axis=-1)
    o = jnp.einsum('bhqk,bkhd->bqhd', p, mem_v.astype(jnp.float32)).astype(h.dtype)
    return o.reshape(b, t, N_XHEADS * X_HEAD_DIM) @ w_xo


def decoder_layer(x, pos, ret_s0, k_past, v_past, mem_k, mem_v, rel_bias, lam_init,
                  n_pre_mix, n_post_mix, n_pre_x, n_post_x, n_pre_ffn, n_post_ffn,
                  w_in, w_out, ret_gn, diff_gn, lam_q1, lam_k1, lam_q2, lam_k2,
                  w_xq, w_xo, w_gate, w_up, w_down):
    b, t, _ = x.shape
    h = rms_norm(x, n_pre_mix)
    rq, rk, rv, rg, dq, dk, dv = jnp.split(h @ w_in, IN_SPLITS, axis=-1)
    rq = rotate(rq.reshape(b, t, N_RET_HEADS, RET_DK), pos)
    rk = rotate(rk.reshape(b, t, N_RET_HEADS, RET_DK), pos) * RET_DK ** -0.5
    rv = rv.reshape(b, t, N_RET_HEADS, RET_DV)
    ret_o, ret_s = retention_chunked(rq, rk, rv, ret_s0)
    ret_o = head_rms_norm(ret_o, ret_gn).reshape(b, t, RET_W).astype(x.dtype)
    ret_o = jax.nn.silu(rg) * ret_o
    dq = dq.reshape(b, t, N_DIFF_HEADS, 2 * DIFF_D)
    dk = dk.reshape(b, t, N_DIFF_HEADS, 2 * DIFF_D)
    dv = dv.reshape(b, t, N_DIFF_HEADS, DIFF_DV)
    lam = (jnp.exp(jnp.sum(lam_q1.astype(jnp.float32) * lam_k1.astype(jnp.float32)))
           - jnp.exp(jnp.sum(lam_q2.astype(jnp.float32) * lam_k2.astype(jnp.float32))) + lam_init)
    if k_past is None:
        qb_len = math.gcd(t, Q_BLOCK)
        nb = t // qb_len
        q_blocks = jnp.moveaxis(dq.reshape(b, nb, qb_len, N_DIFF_HEADS, 2 * DIFF_D), 1, 0)
        pos_blocks = pos.reshape(nb, qb_len)
        o_blocks = lax.map(lambda a: diff_attend(a[0], dk, dv, a[1], pos, lam, rel_bias), (q_blocks, pos_blocks))
        diff_o = jnp.moveaxis(o_blocks, 0, 1).reshape(b, t, N_DIFF_HEADS, DIFF_DV)
    else:
        past = k_past.shape[1]
        keys = jnp.concatenate([k_past, dk], axis=1)
        vals = jnp.concatenate([v_past, dv], axis=1)
        diff_o = diff_attend(dq, keys, vals, pos, jnp.arange(past + t), lam, rel_bias)
    diff_o = (head_rms_norm(diff_o, diff_gn) * (1.0 - lam_init)).reshape(b, t, DIFF_W).astype(x.dtype)
    mix = jnp.concatenate([ret_o, diff_o], axis=-1) @ w_out
    x = x + rms_norm(mix, n_post_mix)
    x = x + rms_norm(cross_attend(rms_norm(x, n_pre_x), mem_k, mem_v, w_xq, w_xo), n_post_x)
    h = rms_norm(x, n_pre_ffn)
    f = (jax.nn.silu(h @ w_gate) * (h @ w_up)) @ w_down
    x = x + rms_norm(f, n_post_ffn)
    return x, dk, dv, ret_s


def setup_inputs(seed: int = 0) -> dict:
    key = jax.random.key(seed)
    ks = iter(jax.random.split(key, 48))
    n_pages = PAST_LEN // PAGE_SIZE
    n_used = DEC_BATCH * n_pages
    n_phys = (n_used * 5 + 3) // 4

    def nrm(shape, scale):
        return jax.random.normal(next(ks), shape, jnp.float32) * scale

    def gain():
        return 1.0 + nrm((DEPTH, D_MODEL), 0.05)

    page_table = jax.random.permutation(next(ks), n_phys)[:n_used].reshape(DEC_BATCH, n_pages).astype(jnp.int32)
    return {
        'x_prompt': nrm((BATCH, SEQ, D_MODEL), 1.0),
        'x_sample': nrm((DEC_BATCH, DEC_SEQ, D_MODEL), 1.0),
        'cache_k': nrm((DEPTH, n_phys, PAGE_SIZE, N_DIFF_HEADS, 2 * DIFF_D), 1.0),
        'cache_v': nrm((DEPTH, n_phys, PAGE_SIZE, N_DIFF_HEADS, DIFF_DV), 1.0),
        'state_ret': nrm((DEPTH, DEC_BATCH, N_RET_HEADS, RET_DK, RET_DV), 1.0),
        'cache_mem_k': nrm((DEPTH, DEC_BATCH, N_MEM, N_XHEADS, X_HEAD_DIM), 1.0),
        'cache_mem_v': nrm((DEPTH, DEC_BATCH, N_MEM, N_XHEADS, X_HEAD_DIM), 1.0),
        'page_table': page_table,
        'mem_prompt': nrm((BATCH, N_MEM, D_MODEL), 1.0),
        'rel_bias': nrm((N_BUCKETS, N_DIFF_HEADS), 0.5),
        'norm_pre_mix': gain(),
        'norm_post_mix': gain(),
        'norm_pre_x': gain(),
        'norm_post_x': gain(),
        'norm_pre_ffn': gain(),
        'norm_post_ffn': gain(),
        'norm_mem': gain(),
        'w_in': nrm((DEPTH, D_MODEL, D_IN), D_MODEL ** -0.5),
        'w_out': nrm((DEPTH, MIX_WIDTH, D_MODEL), MIX_WIDTH ** -0.5),
        'ret_gn': 1.0 + nrm((DEPTH, RET_W), 0.05),
        'diff_gn': 1.0 + nrm((DEPTH, DIFF_W), 0.05),
        'lam_q1': nrm((DEPTH, DIFF_D), 0.1),
        'lam_k1': nrm((DEPTH, DIFF_D), 0.1),
        'lam_q2': nrm((DEPTH, DIFF_D), 0.1),
        'lam_k2': nrm((DEPTH, DIFF_D), 0.1),
        'w_xq': nrm((DEPTH, D_MODEL, N_XHEADS * X_HEAD_DIM), D_MODEL ** -0.5),
        'w_xk': nrm((DEPTH, D_MODEL, N_XHEADS * X_HEAD_DIM), D_MODEL ** -0.5),
        'w_xv': nrm((DEPTH, D_MODEL, N_XHEADS * X_HEAD_DIM), D_MODEL ** -0.5),
        'w_xo': nrm((DEPTH, N_XHEADS * X_HEAD_DIM, D_MODEL), (N_XHEADS * X_HEAD_DIM) ** -0.5),
        'w_gate': nrm((DEPTH, D_MODEL, D_FF), D_MODEL ** -0.5),
        'w_up': nrm((DEPTH, D_MODEL, D_FF), D_MODEL ** -0.5),
        'w_down': nrm((DEPTH, D_FF, D_MODEL), D_FF ** -0.5),
    }


def reference(x_prompt, x_sample, cache_k, cache_v, state_ret, cache_mem_k, cache_mem_v, page_table, mem_prompt,
              rel_bias, norm_pre_mix, norm_post_mix, norm_pre_x, norm_post_x, norm_pre_ffn, norm_post_ffn, norm_mem,
              w_in, w_out, ret_gn, diff_gn, lam_q1, lam_k1, lam_q2, lam_k2,
              w_xq, w_xk, w_xv, w_xo, w_gate, w_up, w_down):
    b_p, t_p, _ = x_prompt.shape
    b_s, t_s, _ = x_sample.shape
    n_mem = mem_prompt.shape[1]
    past_len = page_table.shape[1] * cache_k.shape[2]
    pos_p = jnp.arange(t_p)
    pos_s = past_len + jnp.arange(t_s)
    xp, xs = x_prompt, x_sample
    kp_l, vp_l, sp_l, mkp_l, mvp_l, ks_l, vs_l, ss_l = [], [], [], [], [], [], [], []
    for l in range(DEPTH):
        lam_init = 0.8 - 0.6 * math.exp(-0.3 * l)
        shared = (rel_bias, lam_init, norm_pre_mix[l], norm_post_mix[l], norm_pre_x[l], norm_post_x[l],
                  norm_pre_ffn[l], norm_post_ffn[l], w_in[l], w_out[l], ret_gn[l], diff_gn[l],
                  lam_q1[l], lam_k1[l], lam_q2[l], lam_k2[l], w_xq[l], w_xo[l], w_gate[l], w_up[l], w_down[l])
        mem_n = rms_norm(mem_prompt, norm_mem[l])
        mk_p = (mem_n @ w_xk[l]).reshape(b_p, n_mem, N_XHEADS, X_HEAD_DIM)
        mv_p = (mem_n @ w_xv[l]).reshape(b_p, n_mem, N_XHEADS, X_HEAD_DIM)
        s0 = jnp.zeros((b_p, N_RET_HEADS, RET_DK, RET_DV), jnp.float32)
        xp, k_new_p, v_new_p, s_p = decoder_layer(xp, pos_p, s0, None, None, mk_p, mv_p, *shared)
        k_past = cache_k[l][page_table].reshape(b_s, past_len, N_DIFF_HEADS, 2 * DIFF_D)
        v_past = cache_v[l][page_table].reshape(b_s, past_len, N_DIFF_HEADS, DIFF_DV)
        xs, k_new_s, v_new_s, s_s = decoder_layer(xs, pos_s, state_ret[l], k_past, v_past,
                                                  cache_mem_k[l], cache_mem_v[l], *shared)
        kp_l.append(k_new_p); vp_l.append(v_new_p); sp_l.append(s_p.astype(state_ret.dtype))
        mkp_l.append(mk_p); mvp_l.append(mv_p)
        ks_l.append(k_new_s); vs_l.append(v_new_s); ss_l.append(s_s.astype(state_ret.dtype))
    k_prompt = jnp.stack(kp_l)
    v_prompt = jnp.stack(vp_l)
    ret_prompt = jnp.stack(sp_l)
    mem_k_prompt = jnp.stack(mkp_l)
    mem_v_prompt = jnp.stack(mvp_l)
    k_sample = jnp.stack(ks_l)
    v_sample = jnp.stack(vs_l)
    ret_sample = jnp.stack(ss_l)
    return (xp, xs, k_prompt, v_prompt, ret_prompt, mem_k_prompt, mem_v_prompt, k_sample, v_sample, ret_sample)
```

```python
import functools
import math

import jax
import jax.numpy as jnp
from jax import lax
from jax.experimental import pallas as pl
from jax.experimental.pallas import tpu as pltpu

F32 = jnp.float32
BF16 = jnp.bfloat16

D_MODEL = 2048
N_HEADS = 8
HEAD_W = 128
GROUP_W = N_HEADS * HEAD_W
DIFF_D = 64
ROPE_BASE = 10000.0
RET_CHUNK = 128
N_BUCKETS = 32
MAX_EXACT = N_BUCKETS // 2
MAX_DISTANCE = 128
N_XHEADS = 4
RMS_EPS = 1e-6
NEG_BIG = -1e30
N_SECTIONS_A = 5
VMEM_LIMIT = 56 * 1024 * 1024


def _cparams(sem):
    return pltpu.CompilerParams(dimension_semantics=sem, vmem_limit_bytes=VMEM_LIMIT)


def _rms(x, gain):
    return x * lax.rsqrt(jnp.mean(x * x, axis=-1, keepdims=True) + RMS_EPS) * gain


def _dot(a, b):
    return jnp.dot(a, b, preferred_element_type=F32)


def _dot_nt(a, b):
    return lax.dot_general(a, b, (((1,), (1,)), ((), ())), preferred_element_type=F32)


def _dot_tn(a, b):
    return lax.dot_general(a, b, (((0,), (0,)), ((), ())), preferred_element_type=F32)


def _norm_matmul_kernel(x_ref, g_ref, w_ref, o_ref, h_ref):
    @pl.when(pl.program_id(1) == 0)
    def _():
        h_ref[...] = _rms(x_ref[...], g_ref[...]).astype(BF16)

    o_ref[...] = _dot(h_ref[...], w_ref[...]).astype(o_ref.dtype)


def _norm_matmul(x, gain, w, *, tm, tn, out_dtype, name):
    m, k = x.shape
    n = w.shape[1]
    return pl.pallas_call(
        _norm_matmul_kernel,
        grid=(m // tm, n // tn),
        in_specs=[pl.BlockSpec((tm, k), lambda i, j: (i, 0)),
                  pl.BlockSpec((1, k), lambda i, j: (0, 0)),
                  pl.BlockSpec((k, tn), lambda i, j: (0, j))],
        out_specs=pl.BlockSpec((tm, tn), lambda i, j: (i, j)),
        out_shape=jax.ShapeDtypeStruct((m, n), out_dtype),
        scratch_shapes=[pltpu.VMEM((tm, k), BF16)],
        compiler_params=_cparams(("parallel", "arbitrary")),
        name=name,
    )(x, gain, w)


def _in_proj_kernel(x_ref, g_ref, w_ref, a_ref, dk_ref, dv_ref, h_ref, *, n_a, n_g):
    j = pl.program_id(1)

    @pl.when(j == 0)
    def _():
        h_ref[...] = _rms(x_ref[...], g_ref[...]).astype(BF16)

    res = _dot(h_ref[...], w_ref[...])

    @pl.when(j < n_a)
    def _():
        a_ref[...] = res

    @pl.when(jnp.logical_and(j >= n_a, j < n_a + n_g))
    def _():
        dk_ref[...] = res

    @pl.when(j >= n_a + n_g)
    def _():
        dv_ref[...] = res


def _in_proj(x, gain, w, *, tm, tn, name):
    m, k = x.shape
    n = w.shape[1]
    n_g = GROUP_W // tn
    n_a = N_SECTIONS_A * n_g
    return pl.pallas_call(
        functools.partial(_in_proj_kernel, n_a=n_a, n_g=n_g),
        grid=(m // tm, n // tn),
        in_specs=[pl.BlockSpec((tm, k), lambda i, j: (i, 0)),
                  pl.BlockSpec((1, k), lambda i, j: (0, 0)),
                  pl.BlockSpec((k, tn), lambda i, j: (0, j))],
        out_specs=[pl.BlockSpec((tm, tn), lambda i, j: (i, jnp.minimum(j, n_a - 1))),
                   pl.BlockSpec((tm, tn), lambda i, j: (i, jnp.clip(j - n_a, 0, n_g - 1))),
                   pl.BlockSpec((tm, tn), lambda i, j: (i, jnp.clip(j - n_a - n_g, 0, n_g - 1)))],
        out_shape=[jax.ShapeDtypeStruct((m, N_SECTIONS_A * GROUP_W), F32),
                   jax.ShapeDtypeStruct((m, GROUP_W), F32),
                   jax.ShapeDtypeStruct((m, GROUP_W), F32)],
        scratch_shapes=[pltpu.VMEM((tm, k), BF16)],
        compiler_params=_cparams(("parallel", "arbitrary")),
        name=name,
    )(x, gain, w)


def _rotate(x, cos, sin_signed):
    lane = lax.broadcasted_iota(jnp.int32, x.shape, 1)
    nxt = pltpu.roll(x, HEAD_W - 1, 1)
    prv = pltpu.roll(x, 1, 1)
    return x * cos + jnp.where(lane % 2 == 0, nxt, prv) * sin_signed


def _retention_kernel(q_ref, k_ref, v_ref, g_ref, cos_ref, sin_ref, lg_ref, gn_ref, s0_ref,
                      o_ref, sf_ref, *, chunk, n_chunks, heads):
    rows_mm = RET_CHUNK
    ri = lax.broadcasted_iota(jnp.int32, (rows_mm, rows_mm), 0)
    ci = lax.broadcasted_iota(jnp.int32, (rows_mm, rows_mm), 1)
    rel = (ri - ci).astype(F32)
    idx = lax.broadcasted_iota(jnp.int32, (rows_mm, 1), 0).astype(F32)

    def pad(a):
        if chunk == rows_mm:
            return a
        return jnp.concatenate([a, jnp.zeros((rows_mm - chunk, a.shape[1]), a.dtype)], axis=0)

    for hh in range(heads):
        cols = slice(hh * HEAD_W, (hh + 1) * HEAD_W)
        lg = lg_ref[hh, 0:1, 0:1]
        intra = jnp.where(rel >= 0, jnp.exp(lg * jnp.maximum(rel, 0.0)), 0.0)
        q_dec = jnp.exp(lg * (idx + 1.0))
        k_dec = jnp.exp(lg * jnp.maximum(chunk - 1.0 - idx, 0.0))
        chunk_dec = jnp.exp(lg * float(chunk))
        gn = gn_ref[:, cols]

        def body(c, s, cols=cols, intra=intra, q_dec=q_dec, k_dec=k_dec, chunk_dec=chunk_dec, gn=gn):
            if n_chunks == 1:
                rows = slice(0, chunk)
            else:
                rows = pl.ds(pl.multiple_of(c * chunk, chunk), chunk)
            cos = cos_ref[rows, :]
            sin = sin_ref[rows, :]
            q = pad(_rotate(q_ref[rows, cols], cos, sin))
            k = pad(_rotate(k_ref[rows, cols], cos, sin) * (HEAD_W ** -0.5))
            qb = q.astype(BF16)
            vb = pad(v_ref[rows, cols]).astype(BF16)
            scores = _dot_nt(qb, k.astype(BF16)) * intra
            o = _dot(scores.astype(BF16), vb) + _dot(qb, s.astype(BF16)) * q_dec
            s_new = s * chunk_dec + _dot_tn((k * k_dec).astype(BF16), vb)
            gate = g_ref[rows, cols]
            y = _rms(o[0:chunk, :], gn)
            o_ref[rows, cols] = ((gate * jax.nn.sigmoid(gate)) * y).astype(o_ref.dtype)
            return s_new

        s0 = s0_ref[0, hh]
        if n_chunks == 1:
            s_fin = body(0, s0)
        else:
            s_fin = lax.fori_loop(0, n_chunks, body, s0)
        sf_ref[0, hh] = s_fin


def _retention(proj_a, cos, sin, lg, gn, s0, *, batch, seq, heads, out_dtype, name):
    chunk = math.gcd(seq, RET_CHUNK)
    w = heads * HEAD_W
    n_hb = N_HEADS // heads

    def sec(s):
        return pl.BlockSpec((seq, w), lambda b, h, s=s: (b, s * n_hb + h))

    return pl.pallas_call(
        functools.partial(_retention_kernel, chunk=chunk, n_chunks=seq // chunk, heads=heads),
        grid=(batch, n_hb),
        in_specs=[sec(0), sec(1), sec(2), sec(3),
                  pl.BlockSpec((seq, HEAD_W), lambda b, h: (0, 0)),
                  pl.BlockSpec((seq, HEAD_W), lambda b, h: (0, 0)),
                  pl.BlockSpec((heads, 8, HEAD_W), lambda b, h: (h, 0, 0)),
                  pl.BlockSpec((1, w), lambda b, h: (0, h)),
                  pl.BlockSpec((1, heads, HEAD_W, HEAD_W), lambda b, h: (b, h, 0, 0))],
        out_specs=[pl.BlockSpec((seq, w), lambda b, h: (b, h)),
                   pl.BlockSpec((1, heads, HEAD_W, HEAD_W), lambda b, h: (b, h, 0, 0))],
        out_shape=[jax.ShapeDtypeStruct((batch * seq, GROUP_W), out_dtype),
                   jax.ShapeDtypeStruct((batch, N_HEADS, HEAD_W, HEAD_W), F32)],
        compiler_params=_cparams(("parallel", "parallel")),
        name=name,
    )(proj_a, proj_a, proj_a, proj_a, cos, sin, lg, gn, s0)


def _t5_bucket(qpos, kpos):
    n = jnp.maximum(qpos[:, None] - kpos[None, :], 0)
    nf = jnp.maximum(n, 1).astype(F32)
    large = MAX_EXACT + (jnp.log(nf / MAX_EXACT) / math.log(MAX_DISTANCE / MAX_EXACT)
                         * (N_BUCKETS - MAX_EXACT)).astype(jnp.int32)
    large = jnp.minimum(large, N_BUCKETS - 1)
    return jnp.where(n < MAX_EXACT, n, large)


def _bias_from_buckets(bkt, rb_ref, h):
    far = rb_ref[N_BUCKETS - 1, h]
    b = jnp.zeros(bkt.shape, F32)
    for u in range(N_BUCKETS - 1):
        b = jnp.where(bkt == u, rb_ref[u, h] - far, b)
    return b


def _lam(lamp_ref, lam_init):
    e1 = jnp.exp(jnp.sum(lamp_ref[0:1, :] * lamp_ref[1:2, :], axis=-1, keepdims=True))
    e2 = jnp.exp(jnp.sum(lamp_ref[2:3, :] * lamp_ref[3:4, :], axis=-1, keepdims=True))
    return e1 - e2 + lam_init


def _softmax_step(s, vb, m_ref, l_ref, acc_ref):
    m_prev = m_ref[...]
    m_new = jnp.maximum(m_prev, jnp.max(s, axis=-1, keepdims=True))
    alpha = jnp.exp(m_prev - m_new)
    p = jnp.exp(s - m_new)
    l_ref[...] = alpha * l_ref[...] + jnp.sum(p, axis=-1, keepdims=True)
    acc_ref[...] = alpha * acc_ref[...] + _dot(p.astype(BF16), vb)
    m_ref[...] = m_new


def _diff_prompt_kernel(q_ref, k_ref, v_ref, bkt_ref, rb_ref, lamp_ref, gn_ref, o_ref,
                        kbf, vbf, qs, m_ref, l_ref, acc_ref, bias_ref, *, t, lam_init):
    h = pl.program_id(1)
    qi = pl.program_id(2)

    @pl.when(qi == 0)
    def _():
        kbf[...] = k_ref[...].astype(BF16)
        vbf[...] = v_ref[...].astype(BF16)
        bias_ref[0] = _bias_from_buckets(bkt_ref[0], rb_ref, h)
        bias_ref[1] = _bias_from_buckets(bkt_ref[1], rb_ref, h)

    q = q_ref[...] * (DIFF_D ** -0.5)
    lane = lax.broadcasted_iota(jnp.int32, q.shape, 1)
    qs[0:t, :] = jnp.where(lane < DIFF_D, q, 0.0).astype(BF16)
    qs[t:2 * t, :] = jnp.where(lane >= DIFF_D, q, 0.0).astype(BF16)
    m_ref[...] = jnp.full(m_ref.shape, -jnp.inf, F32)
    l_ref[...] = jnp.zeros(l_ref.shape, F32)
    acc_ref[...] = jnp.zeros(acc_ref.shape, F32)

    def tile(ki, bias, causal):
        rows = pl.ds(pl.multiple_of(ki * t, t), t)
        s = _dot_nt(qs[...], kbf[rows, :])
        if bias is not None:
            s = s + jnp.concatenate([bias, bias], axis=0)
        if causal:
            ri = lax.broadcasted_iota(jnp.int32, (2 * t, t), 0)
            ci = lax.broadcasted_iota(jnp.int32, (2 * t, t), 1)
            s = jnp.where(ci <= jnp.where(ri >= t, ri - t, ri), s, NEG_BIG)
        _softmax_step(s, vbf[rows, :], m_ref, l_ref, acc_ref)

    def far(ki, carry):
        tile(ki, None, False)
        return carry

    lax.fori_loop(0, jnp.maximum(qi - 1, 0), far, 0)

    @pl.when(qi > 0)
    def _():
        tile(qi - 1, bias_ref[1], False)

    tile(qi, bias_ref[0], True)

    lam = _lam(lamp_ref, lam_init)
    o = acc_ref[0:t, :] / l_ref[0:t, :] - lam * (acc_ref[t:2 * t, :] / l_ref[t:2 * t, :])
    o_ref[...] = (_rms(o, gn_ref[...]) * (1.0 - lam_init)).astype(o_ref.dtype)


def _diff_prompt(proj_a, dk, dv, rel_bias, lamp, gn, *, batch, seq, t, lam_init, name):
    nq = seq // t
    pos = jnp.arange(t)
    bkt = jnp.stack([_t5_bucket(pos, pos), _t5_bucket(pos + t, pos)]).astype(jnp.int32)
    q_sec = (N_SECTIONS_A - 1) * N_HEADS
    return pl.pallas_call(
        functools.partial(_diff_prompt_kernel, t=t, lam_init=lam_init),
        grid=(batch, N_HEADS, nq),
        in_specs=[pl.BlockSpec((t, HEAD_W), lambda b, h, i: (b * nq + i, q_sec + h)),
                  pl.BlockSpec((seq, HEAD_W), lambda b, h, i: (b, h)),
                  pl.BlockSpec((seq, HEAD_W), lambda b, h, i: (b, h)),
                  pl.BlockSpec((2, t, t), lambda b, h, i: (0, 0, 0)),
                  pl.BlockSpec(memory_space=pltpu.SMEM),
                  pl.BlockSpec((4, DIFF_D), lambda b, h, i: (0, 0)),
                  pl.BlockSpec((1, HEAD_W), lambda b, h, i: (0, h))],
        out_specs=pl.BlockSpec((t, HEAD_W), lambda b, h, i: (b * nq + i, h)),
        out_shape=jax.ShapeDtypeStruct((batch * seq, GROUP_W), BF16),
        scratch_shapes=[pltpu.VMEM((seq, HEAD_W), BF16), pltpu.VMEM((seq, HEAD_W), BF16),
                        pltpu.VMEM((2 * t, HEAD_W), BF16),
                        pltpu.VMEM((2 * t, 1), F32), pltpu.VMEM((2 * t, 1), F32),
                        pltpu.VMEM((2 * t, HEAD_W), F32),
                        pltpu.VMEM((2, t, t), F32)],
        compiler_params=_cparams(("parallel", "parallel", "arbitrary")),
        name=name,
    )(proj_a, dk, dv, bkt, rel_bias, lamp, gn)


def _diff_sample_kernel(pt_ref, q_ref, kc_ref, vc_ref, kn_ref, vn_ref, bkt_last_ref, bkt_new_ref, rb_ref,
                        lamp_ref, gn_ref, o_ref, qf, qbf, kpad, vpad, m_ref, l_ref, acc_ref,
                        mask_ref, bias_last_ref, bias_new_ref, *, t_new, lam_init):
    del pt_ref
    b = pl.program_id(0)
    p = pl.program_id(1)
    last = pl.num_programs(1) - 1
    rows_per_head = 2 * t_new
    n_new = t_new * N_HEADS

    def per_head_bias(bkt_ref):
        parts = []
        for h in range(N_HEADS):
            r0 = h * rows_per_head
            bk = bkt_ref[r0:r0 + rows_per_head, :]
            parts.append(jnp.where(bk >= 0, _bias_from_buckets(bk, rb_ref, h), NEG_BIG))
        return jnp.concatenate(parts, axis=0)

    @pl.when(jnp.logical_and(b == 0, p == 0))
    def _():
        mask_ref[...] = jnp.where(bkt_last_ref[...] >= 0, 0.0, NEG_BIG)
        bias_last_ref[...] = per_head_bias(bkt_last_ref)
        bias_new_ref[...] = per_head_bias(bkt_new_ref)
        kpad[...] = jnp.zeros(kpad.shape, BF16)
        vpad[...] = jnp.zeros(vpad.shape, BF16)

    @pl.when(p == 0)
    def _():
        q = q_ref[...] * (DIFF_D ** -0.5)
        lane = lax.broadcasted_iota(jnp.int32, (t_new, HEAD_W), 1)
        for h in range(N_HEADS):
            cols = slice(h * HEAD_W, (h + 1) * HEAD_W)
            r0 = h * rows_per_head
            qf[r0:r0 + t_new, :] = jnp.where(lane < DIFF_D, q[:, cols], 0.0)
            qf[r0 + t_new:r0 + 2 * t_new, :] = jnp.where(lane >= DIFF_D, q[:, cols], 0.0)
        qbf[...] = qf[...].astype(BF16)
        m_ref[...] = jnp.full(m_ref.shape, -jnp.inf, F32)
        l_ref[...] = jnp.zeros(l_ref.shape, F32)
        acc_ref[...] = jnp.zeros(acc_ref.shape, F32)

    s = _dot_nt(qbf[...], kc_ref[0].astype(BF16))
    vb = vc_ref[0].astype(BF16)

    @pl.when(p < last)
    def _():
        _softmax_step(s + mask_ref[...], vb, m_ref, l_ref, acc_ref)

    @pl.when(p == last)
    def _():
        _softmax_step(s + bias_last_ref[...], vb, m_ref, l_ref, acc_ref)
        kpad[0:n_new, :] = kn_ref[0].astype(BF16)
        vpad[0:n_new, :] = vn_ref[0].astype(BF16)
        _softmax_step(_dot_nt(qbf[...], kpad[...]) + bias_new_ref[...], vpad[...], m_ref, l_ref, acc_ref)

        lam = _lam(lamp_ref, lam_init)
        for h in range(N_HEADS):
            cols = slice(h * HEAD_W, (h + 1) * HEAD_W)
            r0 = h * rows_per_head
            r1 = r0 + t_new
            o = (acc_ref[r0:r1, :] / l_ref[r0:r1, :]
                 - lam * (acc_ref[r1:r1 + t_new, :] / l_ref[r1:r1 + t_new, :]))
            o_ref[:, cols] = (_rms(o, gn_ref[:, cols]) * (1.0 - lam_init)).astype(o_ref.dtype)


def _diff_sample(proj_a, dk, dv, cache_k, cache_v, page_table, rel_bias, lamp, gn, *, t_new, lam_init, name):
    batch, n_pages = page_table.shape
    n_phys, page = cache_k.shape[0], cache_k.shape[1]
    past = n_pages * page
    rows = N_HEADS * 2 * t_new
    kv_rows = page * N_HEADS
    new_rows = 128
    assert t_new * N_HEADS <= new_rows
    q_head = jnp.repeat(jnp.arange(N_HEADS), 2 * t_new)
    q_pos = past + jnp.tile(jnp.arange(t_new), 2 * N_HEADS)

    def buckets(n_cols, first_key_pos, n_valid_cols):
        col = jnp.arange(n_cols)
        k_pos = first_key_pos + col // N_HEADS
        ok = ((col % N_HEADS)[None, :] == q_head[:, None]) & (k_pos[None, :] <= q_pos[:, None])
        ok = ok & (col < n_valid_cols)[None, :]
        return jnp.where(ok, _t5_bucket(q_pos, k_pos), -1).astype(jnp.int32)

    bkt_last = buckets(kv_rows, past - page, kv_rows)
    bkt_new = buckets(new_rows, past, t_new * N_HEADS)
    grid_spec = pltpu.PrefetchScalarGridSpec(
        num_scalar_prefetch=1,
        grid=(batch, n_pages),
        in_specs=[pl.BlockSpec((t_new, GROUP_W), lambda b, p, pt: (b, N_SECTIONS_A - 1)),
                  pl.BlockSpec((1, kv_rows, HEAD_W), lambda b, p, pt: (pt[b, p], 0, 0)),
                  pl.BlockSpec((1, kv_rows, HEAD_W), lambda b, p, pt: (pt[b, p], 0, 0)),
                  pl.BlockSpec((1, t_new * N_HEADS, HEAD_W), lambda b, p, pt: (b, 0, 0)),
                  pl.BlockSpec((1, t_new * N_HEADS, HEAD_W), lambda b, p, pt: (b, 0, 0)),
                  pl.BlockSpec((rows, kv_rows), lambda b, p, pt: (0, 0)),
                  pl.BlockSpec((rows, new_rows), lambda b, p, pt: (0, 0)),
                  pl.BlockSpec(memory_space=pltpu.SMEM),
                  pl.BlockSpec((4, DIFF_D), lambda b, p, pt: (0, 0)),
                  pl.BlockSpec((1, GROUP_W), lambda b, p, pt: (0, 0))],
        out_specs=pl.BlockSpec((t_new, GROUP_W), lambda b, p, pt: (b, 0)),
        scratch_shapes=[pltpu.VMEM((rows, HEAD_W), F32), pltpu.VMEM((rows, HEAD_W), BF16),
                        pltpu.VMEM((new_rows, HEAD_W), BF16), pltpu.VMEM((new_rows, HEAD_W), BF16),
                        pltpu.VMEM((rows, 1), F32), pltpu.VMEM((rows, 1), F32),
                        pltpu.VMEM((rows, HEAD_W), F32),
                        pltpu.VMEM((rows, kv_rows), F32), pltpu.VMEM((rows, kv_rows), F32),
                        pltpu.VMEM((rows, new_rows), F32)],
    )
    return pl.pallas_call(
        functools.partial(_diff_sample_kernel, t_new=t_new, lam_init=lam_init),
        grid_spec=grid_spec,
        out_shape=jax.ShapeDtypeStruct((batch * t_new, GROUP_W), F32),
        compiler_params=_cparams(("arbitrary", "arbitrary")),
        name=name,
    )(page_table, proj_a, cache_k.reshape(n_phys, kv_rows, HEAD_W), cache_v.reshape(n_phys, kv_rows, HEAD_W),
      dk.reshape(batch, t_new * N_HEADS, HEAD_W), dv.reshape(batch, t_new * N_HEADS, HEAD_W),
      bkt_last, bkt_new, rel_bias, lamp, gn)


def _proj_norm_res_kernel(*refs, n_in):
    a_refs, w_refs = refs[:n_in], refs[n_in:2 * n_in]
    gain_ref, x_ref, o_ref = refs[2 * n_in:]
    y = _dot(a_refs[0][...].astype(BF16), w_refs[0][...])
    for a_ref, w_ref in zip(a_refs[1:], w_refs[1:]):
        y = y + _dot(a_ref[...].astype(BF16), w_ref[...])
    o_ref[...] = x_ref[...] + _rms(y, gain_ref[...])


def _proj_norm_res(a_list, w, gain, x, *, tm, name):
    m, n = x.shape
    n_in = len(a_list)
    ka = a_list[0].shape[1]
    return pl.pallas_call(
        functools.partial(_proj_norm_res_kernel, n_in=n_in),
        grid=(m // tm,),
        in_specs=([pl.BlockSpec((tm, ka), lambda i: (i, 0)) for _ in range(n_in)]
                  + [pl.BlockSpec((ka, n), lambda i, s=s: (s, 0)) for s in range(n_in)]
                  + [pl.BlockSpec((1, n), lambda i: (0, 0)), pl.BlockSpec((tm, n), lambda i: (i, 0))]),
        out_specs=pl.BlockSpec((tm, n), lambda i: (i, 0)),
        out_shape=jax.ShapeDtypeStruct((m, n), F32),
        compiler_params=_cparams(("parallel",)),
        name=name,
    )(*a_list, *([w] * n_in), gain, x)


def _cross_kernel(q_ref, mk_ref, mv_ref, o_ref):
    for h in range(N_XHEADS):
        cols = slice(h * HEAD_W, (h + 1) * HEAD_W)
        s = _dot_nt(q_ref[0, :, cols].astype(BF16), mk_ref[0, :, cols].astype(BF16)) * (HEAD_W ** -0.5)
        s = s - jnp.max(s, axis=-1, keepdims=True)
        e = jnp.exp(s)
        p = e / jnp.sum(e, axis=-1, keepdims=True)
        o_ref[0, :, cols] = _dot(p.astype(BF16), mv_ref[0, :, cols].astype(BF16)).astype(o_ref.dtype)


def _cross_attend(q, mk, mv, *, tq, out_dtype, name):
    b, t, w = q.shape
    n_mem = mk.shape[1]
    return pl.pallas_call(
        _cross_kernel,
        grid=(b, t // tq),
        in_specs=[pl.BlockSpec((1, tq, w), lambda i, j: (i, j, 0)),
                  pl.BlockSpec((1, n_mem, w), lambda i, j: (i, 0, 0)),
                  pl.BlockSpec((1, n_mem, w), lambda i, j: (i, 0, 0))],
        out_specs=pl.BlockSpec((1, tq, w), lambda i, j: (i, j, 0)),
        out_shape=jax.ShapeDtypeStruct((b, t, w), out_dtype),
        compiler_params=_cparams(("parallel", "parallel")),
        name=name,
    )(q, mk, mv)


def _ffn_kernel(x_ref, gpre_ref, wg_ref, wu_ref, wd_ref, gpost_ref, o_ref, h_ref, acc_ref):
    f = pl.program_id(1)

    @pl.when(f == 0)
    def _():
        h_ref[...] = _rms(x_ref[...], gpre_ref[...]).astype(BF16)
        acc_ref[...] = jnp.zeros(acc_ref.shape, F32)

    h = h_ref[...]
    g = _dot(h, wg_ref[...])
    a = (g * jax.nn.sigmoid(g)) * _dot(h, wu_ref[...])
    acc_ref[...] += _dot(a.astype(BF16), wd_ref[...])

    @pl.when(f == pl.num_programs(1) - 1)
    def _():
        o_ref[...] = x_ref[...] + _rms(acc_ref[...], gpost_ref[...])


def _ffn(x, gpre, wg, wu, wd, gpost, *, tm, tf, name):
    m, d = x.shape
    dff = wg.shape[1]
    return pl.pallas_call(
        _ffn_kernel,
        grid=(m // tm, dff // tf),
        in_specs=[pl.BlockSpec((tm, d), lambda i, f: (i, 0)),
                  pl.BlockSpec((1, d), lambda i, f: (0, 0)),
                  pl.BlockSpec((d, tf), lambda i, f: (0, f)),
                  pl.BlockSpec((d, tf), lambda i, f: (0, f)),
                  pl.BlockSpec((tf, d), lambda i, f: (f, 0)),
                  pl.BlockSpec((1, d), lambda i, f: (0, 0))],
        out_specs=pl.BlockSpec((tm, d), lambda i, f: (i, 0)),
        out_shape=jax.ShapeDtypeStruct((m, d), F32),
        scratch_shapes=[pltpu.VMEM((tm, d), BF16), pltpu.VMEM((tm, d), F32)],
        compiler_params=_cparams(("parallel", "arbitrary")),
        name=name,
    )(x, gpre, wg, wu, wd, gpost)


def _rope_tables(pos):
    inv_freq = ROPE_BASE ** (-jnp.arange(0, HEAD_W, 2, dtype=F32) / HEAD_W)
    ang = pos.astype(F32)[:, None] * inv_freq[None, :]
    cos = jnp.repeat(jnp.cos(ang), 2, axis=-1)
    sin = jnp.sin(ang)
    return cos, jnp.stack([-sin, sin], axis=-1).reshape(cos.shape)


def _layer(x, pos, s0, mem_k, mem_v, paged, w, *, batch, seq, lam_init, tag):
    m = batch * seq
    big = paged is None
    tm = 512 if big else m
    proj_a, dk, dv = _in_proj(x, w["n_pre_mix"], w["w_in"], tm=1024 if big else m, tn=512, name=f"in_proj_{tag}")
    cos, sin = _rope_tables(pos)
    ret_o, ret_s = _retention(proj_a, cos, sin, w["lg"], w["ret_gn"], s0, batch=batch, seq=seq,
                              heads=1 if big else N_HEADS, out_dtype=BF16 if big else F32, name=f"retention_{tag}")
    if big:
        diff_o = _diff_prompt(proj_a, dk, dv, w["rel_bias"], w["lamp"], w["diff_gn"], batch=batch, seq=seq,
                              t=512, lam_init=lam_init, name=f"diff_attn_{tag}")
    else:
        diff_o = _diff_sample(proj_a, dk, dv, *paged, w["rel_bias"], w["lamp"], w["diff_gn"], t_new=seq,
                              lam_init=lam_init, name=f"diff_attn_{tag}")
    x = _proj_norm_res([ret_o, diff_o], w["w_out"], w["n_post_mix"], x, tm=tm, name=f"out_proj_{tag}")
    xw = N_XHEADS * HEAD_W
    q = _norm_matmul(x, w["n_pre_x"], w["w_xq"], tm=tm, tn=xw, out_dtype=BF16 if big else F32, name=f"xq_{tag}")
    o = _cross_attend(q.reshape(batch, seq, xw), mem_k, mem_v, tq=512 if big else seq,
                      out_dtype=BF16 if big else F32, name=f"cross_{tag}")
    x = _proj_norm_res([o.reshape(m, xw)], w["w_xo"], w["n_post_x"], x, tm=tm, name=f"xo_{tag}")
    x = _ffn(x, w["n_pre_ffn"], w["w_gate"], w["w_up"], w["w_down"], w["n_post_ffn"], tm=tm, tf=512,
             name=f"ffn_{tag}")
    return x, dk, dv, ret_s


def kernel(x_prompt, x_sample, cache_k, cache_v, state_ret, cache_mem_k, cache_mem_v, page_table, mem_prompt,
           rel_bias, norm_pre_mix, norm_post_mix, norm_pre_x, norm_post_x, norm_pre_ffn, norm_post_ffn, norm_mem,
           w_in, w_out, ret_gn, diff_gn, lam_q1, lam_k1, lam_q2, lam_k2,
           w_xq, w_xk, w_xv, w_xo, w_gate, w_up, w_down):
    b_p, t_p, d = x_prompt.shape
    b_s, t_s, _ = x_sample.shape
    depth = w_in.shape[0]
    n_mem = mem_prompt.shape[1]
    xw = N_XHEADS * HEAD_W
    past_len = page_table.shape[1] * cache_k.shape[2]
    pos_p = jnp.arange(t_p)
    pos_s = past_len + jnp.arange(t_s)
    log_g = jnp.log1p(-(2.0 ** (-5.0 - jnp.arange(N_HEADS, dtype=F32))))
    lg = jnp.broadcast_to(log_g[:, None, None], (N_HEADS, 8, HEAD_W))
    xp = x_prompt.reshape(b_p * t_p, d)
    xs = x_sample.reshape(b_s * t_s, d)
    outs = [[] for _ in range(8)]
    for l in range(depth):
        lam_init = 0.8 - 0.6 * math.exp(-0.3 * l)
        w = {
            "n_pre_mix": norm_pre_mix[l][None], "n_post_mix": norm_post_mix[l][None],
            "n_pre_x": norm_pre_x[l][None], "n_post_x": norm_post_x[l][None],
            "n_pre_ffn": norm_pre_ffn[l][None], "n_post_ffn": norm_post_ffn[l][None],
            "w_in": w_in[l].astype(BF16), "w_out": w_out[l].astype(BF16),
            "ret_gn": ret_gn[l][None], "diff_gn": diff_gn[l][None],
            "lamp": jnp.stack([lam_q1[l], lam_k1[l], lam_q2[l], lam_k2[l]]),
            "w_xq": w_xq[l].astype(BF16), "w_xo": w_xo[l].astype(BF16),
            "w_gate": w_gate[l].astype(BF16), "w_up": w_up[l].astype(BF16), "w_down": w_down[l].astype(BF16),
            "rel_bias": rel_bias, "lg": lg,
        }
        mem_flat = mem_prompt.reshape(b_p * n_mem, d)
        mk_p = _norm_matmul(mem_flat, norm_mem[l][None], w_xk[l].astype(BF16), tm=b_p * n_mem, tn=xw,
                            out_dtype=F32, name="mem_k")
        mv_p = _norm_matmul(mem_flat, norm_mem[l][None], w_xv[l].astype(BF16), tm=b_p * n_mem, tn=xw,
                            out_dtype=F32, name="mem_v")
        s0 = jnp.zeros((b_p, N_HEADS, HEAD_W, HEAD_W), F32)
        xp, dk_p, dv_p, s_p = _layer(xp, pos_p, s0, mk_p.reshape(b_p, n_mem, xw), mv_p.reshape(b_p, n_mem, xw),
                                     None, w, batch=b_p, seq=t_p, lam_init=lam_init, tag="prompt")
        xs, dk_s, dv_s, s_s = _layer(xs, pos_s, state_ret[l], cache_mem_k[l].reshape(b_s, n_mem, xw),
                                     cache_mem_v[l].reshape(b_s, n_mem, xw),
                                     (cache_k[l], cache_v[l], page_table), w,
                                     batch=b_s, seq=t_s, lam_init=lam_init, tag="sample")
        per_layer = (dk_p.reshape(b_p, t_p, N_HEADS, HEAD_W), dv_p.reshape(b_p, t_p, N_HEADS, HEAD_W), s_p,
                     mk_p.reshape(b_p, n_mem, N_XHEADS, HEAD_W), mv_p.reshape(b_p, n_mem, N_XHEADS, HEAD_W),
                     dk_s.reshape(b_s, t_s, N_HEADS, HEAD_W), dv_s.reshape(b_s, t_s, N_HEADS, HEAD_W), s_s)
        for acc, val in zip(outs, per_layer):
            acc.append(val)
    stacked = [jnp.stack(o) for o in outs]
    return (xp.reshape(b_p, t_p, d), xs.reshape(b_s, t_s, d), *stacked)
```

```python
import functools
import math

import jax
import jax.numpy as jnp
from jax import lax
from jax.experimental import pallas as pl
from jax.experimental.pallas import tpu as pltpu

F32 = jnp.float32
BF16 = jnp.bfloat16

D_MODEL = 2048
N_HEADS = 8
HEAD_W = 128
GROUP_W = N_HEADS * HEAD_W
DIFF_D = 64
ROPE_BASE = 10000.0
RET_CHUNK = 128
N_BUCKETS = 32
MAX_EXACT = N_BUCKETS // 2
MAX_DISTANCE = 128
N_XHEADS = 4
RMS_EPS = 1e-6
NEG_BIG = -1e30
LOG2E = math.log2(math.e)
PROMPT_RB = 32
SAMPLE_RB = 16
PAGES_PER_STEP = 8
RET_UNROLL = 8
N_SECTIONS_A = 5
VMEM_LIMIT = 56 * 1024 * 1024


def _cparams(sem):
    return pltpu.CompilerParams(dimension_semantics=sem, vmem_limit_bytes=VMEM_LIMIT)


def _rms(x, gain):
    return x * lax.rsqrt(jnp.mean(x * x, axis=-1, keepdims=True) + RMS_EPS) * gain


def _dot(a, b):
    return jnp.dot(a, b, preferred_element_type=F32)


def _dot_nt(a, b):
    return lax.dot_general(a, b, (((1,), (1,)), ((), ())), preferred_element_type=F32)


def _dot_tn(a, b):
    return lax.dot_general(a, b, (((0,), (0,)), ((), ())), preferred_element_type=F32)


def _norm_matmul_kernel(x_ref, g_ref, w_ref, o_ref, h_ref):
    @pl.when(pl.program_id(1) == 0)
    def _():
        h_ref[...] = _rms(x_ref[...], g_ref[...]).astype(BF16)

    o_ref[...] = _dot(h_ref[...], w_ref[...]).astype(o_ref.dtype)


def _norm_matmul(x, gain, w, *, tm, tn, out_dtype, name):
    m, k = x.shape
    n = w.shape[1]
    return pl.pallas_call(
        _norm_matmul_kernel,
        grid=(m // tm, n // tn),
        in_specs=[pl.BlockSpec((tm, k), lambda i, j: (i, 0)),
                  pl.BlockSpec((1, k), lambda i, j: (0, 0)),
                  pl.BlockSpec((k, tn), lambda i, j: (0, j))],
        out_specs=pl.BlockSpec((tm, tn), lambda i, j: (i, j)),
        out_shape=jax.ShapeDtypeStruct((m, n), out_dtype),
        scratch_shapes=[pltpu.VMEM((tm, k), BF16)],
        compiler_params=_cparams(("parallel", "arbitrary")),
        name=name,
    )(x, gain, w)


def _in_proj_kernel(x_ref, g_ref, w_ref, a_ref, dk_ref, dv_ref, h_ref, *, n_a, n_g):
    j = pl.program_id(1)

    @pl.when(j == 0)
    def _():
        h_ref[...] = _rms(x_ref[...], g_ref[...]).astype(BF16)

    res = _dot(h_ref[...], w_ref[...])

    @pl.when(j < n_a)
    def _():
        a_ref[...] = res

    @pl.when(jnp.logical_and(j >= n_a, j < n_a + n_g))
    def _():
        dk_ref[...] = res

    @pl.when(j >= n_a + n_g)
    def _():
        dv_ref[...] = res


def _in_proj(x, gain, w, *, tm, tn, name):
    m, k = x.shape
    n = w.shape[1]
    n_g = GROUP_W // tn
    n_a = N_SECTIONS_A * n_g
    return pl.pallas_call(
        functools.partial(_in_proj_kernel, n_a=n_a, n_g=n_g),
        grid=(m // tm, n // tn),
        in_specs=[pl.BlockSpec((tm, k), lambda i, j: (i, 0)),
                  pl.BlockSpec((1, k), lambda i, j: (0, 0)),
                  pl.BlockSpec((k, tn), lambda i, j: (0, j))],
        out_specs=[pl.BlockSpec((tm, tn), lambda i, j: (i, jnp.minimum(j, n_a - 1))),
                   pl.BlockSpec((tm, tn), lambda i, j: (i, jnp.clip(j - n_a, 0, n_g - 1))),
                   pl.BlockSpec((tm, tn), lambda i, j: (i, jnp.clip(j - n_a - n_g, 0, n_g - 1)))],
        out_shape=[jax.ShapeDtypeStruct((m, N_SECTIONS_A * GROUP_W), F32),
                   jax.ShapeDtypeStruct((m, GROUP_W), F32),
                   jax.ShapeDtypeStruct((m, GROUP_W), F32)],
        scratch_shapes=[pltpu.VMEM((tm, k), BF16)],
        compiler_params=_cparams(("parallel", "arbitrary")),
        name=name,
    )(x, gain, w)


def _rotate(x, cos, sin_signed):
    lane = lax.broadcasted_iota(jnp.int32, x.shape, 1)
    nxt = pltpu.roll(x, HEAD_W - 1, 1)
    prv = pltpu.roll(x, 1, 1)
    return x * cos + jnp.where(lane % 2 == 0, nxt, prv) * sin_signed


def _retention_kernel(q_ref, k_ref, v_ref, g_ref, cos_ref, sin_ref, lg_ref, gn_ref, s0_ref,
                      o_ref, sf_ref, *, chunk, n_chunks, heads):
    rows_mm = RET_CHUNK
    ri = lax.broadcasted_iota(jnp.int32, (rows_mm, rows_mm), 0)
    ci = lax.broadcasted_iota(jnp.int32, (rows_mm, rows_mm), 1)
    rel = (ri - ci).astype(F32)
    idx = lax.broadcasted_iota(jnp.int32, (rows_mm, 1), 0).astype(F32)

    def pad(a):
        if chunk == rows_mm:
            return a
        return jnp.concatenate([a, jnp.zeros((rows_mm - chunk, a.shape[1]), a.dtype)], axis=0)

    for hh in range(heads):
        cols = slice(hh * HEAD_W, (hh + 1) * HEAD_W)
        lg = lg_ref[hh, 0:1, 0:1]
        intra = jnp.where(rel >= 0, jnp.exp(lg * jnp.maximum(rel, 0.0)), 0.0)
        q_dec = jnp.exp(lg * (idx + 1.0))
        k_dec = jnp.exp(lg * jnp.maximum(chunk - 1.0 - idx, 0.0))
        chunk_dec = jnp.exp(lg * float(chunk))
        gn = gn_ref[:, cols]

        def body(c, s, cols=cols, intra=intra, q_dec=q_dec, k_dec=k_dec, chunk_dec=chunk_dec, gn=gn):
            if n_chunks == 1:
                rows = slice(0, chunk)
            else:
                rows = pl.ds(pl.multiple_of(c * chunk, chunk), chunk)
            cos = cos_ref[rows, :]
            sin = sin_ref[rows, :]
            q = pad(_rotate(q_ref[rows, cols], cos, sin))
            k = pad(_rotate(k_ref[rows, cols], cos, sin) * (HEAD_W ** -0.5))
            qb = q.astype(BF16)
            vb = pad(v_ref[rows, cols]).astype(BF16)
            scores = _dot_nt(qb, k.astype(BF16)) * intra
            o = _dot(scores.astype(BF16), vb) + _dot(qb, s.astype(BF16)) * q_dec
            s_new = s * chunk_dec + _dot_tn((k * k_dec).astype(BF16), vb)
            gate = g_ref[rows, cols]
            y = _rms(o[0:chunk, :], gn)
            o_ref[rows, cols] = ((gate * jax.nn.sigmoid(gate)) * y).astype(o_ref.dtype)
            return s_new

        s0 = s0_ref[0, hh]
        if n_chunks == 1:
            s_fin = body(0, s0)
        else:
            s_fin = lax.fori_loop(0, n_chunks, body, s0, unroll=math.gcd(n_chunks, RET_UNROLL))
        sf_ref[0, hh] = s_fin


def _retention(proj_a, cos, sin, lg, gn, s0, *, batch, seq, heads, out_dtype, name):
    chunk = math.gcd(seq, RET_CHUNK)
    w = heads * HEAD_W
    n_hb = N_HEADS // heads

    def sec(s):
        return pl.BlockSpec((seq, w), lambda b, h, s=s: (b, s * n_hb + h))

    return pl.pallas_call(
        functools.partial(_retention_kernel, chunk=chunk, n_chunks=seq // chunk, heads=heads),
        grid=(batch, n_hb),
        in_specs=[sec(0), sec(1), sec(2), sec(3),
                  pl.BlockSpec((seq, HEAD_W), lambda b, h: (0, 0)),
                  pl.BlockSpec((seq, HEAD_W), lambda b, h: (0, 0)),
                  pl.BlockSpec((heads, 8, HEAD_W), lambda b, h: (h, 0, 0)),
                  pl.BlockSpec((1, w), lambda b, h: (0, h)),
                  pl.BlockSpec((1, heads, HEAD_W, HEAD_W), lambda b, h: (b, h, 0, 0))],
        out_specs=[pl.BlockSpec((seq, w), lambda b, h: (b, h)),
                   pl.BlockSpec((1, heads, HEAD_W, HEAD_W), lambda b, h: (b, h, 0, 0))],
        out_shape=[jax.ShapeDtypeStruct((batch * seq, GROUP_W), out_dtype),
                   jax.ShapeDtypeStruct((batch, N_HEADS, HEAD_W, HEAD_W), F32)],
        compiler_params=_cparams(("parallel", "parallel")),
        name=name,
    )(proj_a, proj_a, proj_a, proj_a, cos, sin, lg, gn, s0)


def _t5_bucket(qpos, kpos):
    n = jnp.maximum(qpos[:, None] - kpos[None, :], 0)
    nf = jnp.maximum(n, 1).astype(F32)
    large = MAX_EXACT + (jnp.log(nf / MAX_EXACT) / math.log(MAX_DISTANCE / MAX_EXACT)
                         * (N_BUCKETS - MAX_EXACT)).astype(jnp.int32)
    large = jnp.minimum(large, N_BUCKETS - 1)
    return jnp.where(n < MAX_EXACT, n, large)


def _bias_from_buckets(bkt, rb_ref, h):
    far = rb_ref[N_BUCKETS - 1, h]
    b = jnp.zeros(bkt.shape, F32)
    for u in range(N_BUCKETS - 1):
        b = jnp.where(bkt == u, rb_ref[u, h] - far, b)
    return b


def _lam(lamp_ref, lam_init):
    e1 = jnp.exp(jnp.sum(lamp_ref[0:1, :] * lamp_ref[1:2, :], axis=-1, keepdims=True))
    e2 = jnp.exp(jnp.sum(lamp_ref[2:3, :] * lamp_ref[3:4, :], axis=-1, keepdims=True))
    return e1 - e2 + lam_init


def _softmax_tile(s_ref, p_ref, m_ref, alpha_ref, add_fn, *, rb, keep_in_regs):
    n_rows, n_cols = s_ref.shape
    n_slabs = n_cols // HEAD_W
    for r0 in range(0, n_rows, rb):
        r = slice(r0, r0 + rb)
        m_prev = m_ref[r, :]
        vals, part = [], None
        for j in range(n_slabs):
            c = slice(j * HEAD_W, (j + 1) * HEAD_W)
            s = s_ref[r, c]
            if add_fn is not None:
                s = s + add_fn(r0, j)
                if not keep_in_regs:
                    s_ref[r, c] = s
            if keep_in_regs:
                vals.append(s)
            part = s if part is None else jnp.maximum(part, s)
        m_new = jnp.maximum(m_prev, jnp.max(part, axis=-1, keepdims=True))
        for j in range(n_slabs):
            c = slice(j * HEAD_W, (j + 1) * HEAD_W)
            s = vals[j] if keep_in_regs else s_ref[r, c]
            p_ref[r, c] = jnp.exp2(s - m_new).astype(BF16)
        alpha_ref[r, :] = jnp.exp2(m_prev - m_new)
        m_ref[r, :] = m_new


def _accumulate(acc_ref, alpha_ref, p_ref, v1_parts):
    a = alpha_ref[...]
    step = p_ref.shape[1] // len(v1_parts)
    pv = None
    for i, v1 in enumerate(v1_parts):
        d = _dot(p_ref[:, i * step:(i + 1) * step], v1)
        pv = d if pv is None else pv + d
    acc_ref[...] = jnp.concatenate([a, a], axis=1) * acc_ref[...] + pv


def _diff_prompt_kernel(q_ref, k_ref, v_ref, bkt_ref, rb_ref, lamp_ref, gn_ref, o_ref,
                        kbf, v1, qs, s_ref, p_ref, m_ref, alpha_ref, acc_ref, bias_ref, *, t, lam_init):
    h = pl.program_id(1)
    qi = pl.program_id(2)

    @pl.when(qi == 0)
    def _():
        kbf[...] = k_ref[...].astype(BF16)
        v1[:, 0:HEAD_W] = v_ref[...].astype(BF16)
        v1[:, HEAD_W:2 * HEAD_W] = jnp.ones((v1.shape[0], HEAD_W), BF16)
        ri = lax.broadcasted_iota(jnp.int32, (t, t), 0)
        ci = lax.broadcasted_iota(jnp.int32, (t, t), 1)
        bias_ref[0] = jnp.where(ci <= ri, _bias_from_buckets(bkt_ref[0], rb_ref, h) * LOG2E, NEG_BIG)
        bias_ref[1] = _bias_from_buckets(bkt_ref[1], rb_ref, h) * LOG2E

    q = q_ref[...] * (DIFF_D ** -0.5 * LOG2E)
    lane = lax.broadcasted_iota(jnp.int32, q.shape, 1)
    qs[0:t, :] = jnp.where(lane < DIFF_D, q, 0.0).astype(BF16)
    qs[t:2 * t, :] = jnp.where(lane >= DIFF_D, q, 0.0).astype(BF16)
    m_ref[...] = jnp.full(m_ref.shape, -jnp.inf, F32)
    acc_ref[...] = jnp.zeros(acc_ref.shape, F32)

    def tile(ki, which_bias):
        rows = pl.ds(pl.multiple_of(ki * t, t), t)
        s_ref[...] = _dot_nt(qs[...], kbf[rows, :])
        add_fn = None
        if which_bias is not None:
            def add_fn(r0, j):
                q0 = r0 % t
                return bias_ref[which_bias, q0:q0 + PROMPT_RB, j * HEAD_W:(j + 1) * HEAD_W]
        _softmax_tile(s_ref, p_ref, m_ref, alpha_ref, add_fn, rb=PROMPT_RB, keep_in_regs=True)
        _accumulate(acc_ref, alpha_ref, p_ref, [v1[rows, :]])

    def far(ki, carry):
        tile(ki, None)
        return carry

    lax.fori_loop(0, jnp.maximum(qi - 1, 0), far, 0)

    @pl.when(qi > 0)
    def _():
        tile(qi - 1, 1)

    tile(qi, 0)

    lam = _lam(lamp_ref, lam_init)
    o = (acc_ref[0:t, 0:HEAD_W] / acc_ref[0:t, HEAD_W:2 * HEAD_W]
         - lam * (acc_ref[t:2 * t, 0:HEAD_W] / acc_ref[t:2 * t, HEAD_W:2 * HEAD_W]))
    o_ref[...] = (_rms(o, gn_ref[...]) * (1.0 - lam_init)).astype(o_ref.dtype)


def _diff_prompt(proj_a, dk, dv, rel_bias, lamp, gn, *, batch, seq, t, lam_init, name):
    nq = seq // t
    pos = jnp.arange(t)
    bkt = jnp.stack([_t5_bucket(pos, pos), _t5_bucket(pos + t, pos)]).astype(jnp.int32)
    q_sec = (N_SECTIONS_A - 1) * N_HEADS
    return pl.pallas_call(
        functools.partial(_diff_prompt_kernel, t=t, lam_init=lam_init),
        grid=(batch, N_HEADS, nq),
        in_specs=[pl.BlockSpec((t, HEAD_W), lambda b, h, i: (b * nq + i, q_sec + h)),
                  pl.BlockSpec((seq, HEAD_W), lambda b, h, i: (b, h)),
                  pl.BlockSpec((seq, HEAD_W), lambda b, h, i: (b, h)),
                  pl.BlockSpec((2, t, t), lambda b, h, i: (0, 0, 0)),
                  pl.BlockSpec(memory_space=pltpu.SMEM),
                  pl.BlockSpec((4, DIFF_D), lambda b, h, i: (0, 0)),
                  pl.BlockSpec((1, HEAD_W), lambda b, h, i: (0, h))],
        out_specs=pl.BlockSpec((t, HEAD_W), lambda b, h, i: (b * nq + i, h)),
        out_shape=jax.ShapeDtypeStruct((batch * seq, GROUP_W), BF16),
        scratch_shapes=[pltpu.VMEM((seq, HEAD_W), BF16), pltpu.VMEM((seq, 2 * HEAD_W), BF16),
                        pltpu.VMEM((2 * t, HEAD_W), BF16),
                        pltpu.VMEM((2 * t, t), F32), pltpu.VMEM((2 * t, t), BF16),
                        pltpu.VMEM((2 * t, HEAD_W), F32), pltpu.VMEM((2 * t, HEAD_W), F32),
                        pltpu.VMEM((2 * t, 2 * HEAD_W), F32),
                        pltpu.VMEM((2, t, t), F32)],
        compiler_params=_cparams(("parallel", "parallel", "arbitrary")),
        name=name,
    )(proj_a, dk, dv, bkt, rel_bias, lamp, gn)


def _diff_sample_kernel(pt_ref, q_ref, *refs, t_new, lam_init, n_pg):
    del pt_ref
    k_refs, v_refs = refs[:n_pg], refs[n_pg:2 * n_pg]
    (kn_ref, vn_ref, bkt_last_ref, bkt_new_ref, rb_ref, lamp_ref, gn_ref, o_ref,
     qf, qbf, v1all, knew, v1new, s_ref, p_ref, sn_ref, pn_ref, m_ref, alpha_ref, acc_ref,
     mask_ref, bias_last_ref, bias_new_ref) = refs[2 * n_pg:]
    b = pl.program_id(0)
    p = pl.program_id(1)
    last = pl.num_programs(1) - 1
    rows_per_head = 2 * t_new
    n_new = t_new * N_HEADS
    pg_rows = mask_ref.shape[1]
    pg_slabs = pg_rows // HEAD_W

    def per_head_bias(bkt_ref):
        parts = []
        for h in range(N_HEADS):
            r0 = h * rows_per_head
            bk = bkt_ref[r0:r0 + rows_per_head, :]
            parts.append(jnp.where(bk >= 0, _bias_from_buckets(bk, rb_ref, h) * LOG2E, NEG_BIG))
        return jnp.concatenate(parts, axis=0)

    @pl.when(jnp.logical_and(b == 0, p == 0))
    def _():
        mask_ref[...] = jnp.where(bkt_last_ref[...] >= 0, 0.0, NEG_BIG)
        bias_last_ref[...] = per_head_bias(bkt_last_ref)
        bias_new_ref[...] = per_head_bias(bkt_new_ref)
        knew[...] = jnp.zeros(knew.shape, BF16)
        v1new[:, 0:HEAD_W] = jnp.zeros((v1new.shape[0], HEAD_W), BF16)
        v1new[:, HEAD_W:2 * HEAD_W] = jnp.ones((v1new.shape[0], HEAD_W), BF16)
        v1all[:, HEAD_W:2 * HEAD_W] = jnp.ones((v1all.shape[0], HEAD_W), BF16)

    @pl.when(p == 0)
    def _():
        q = q_ref[...] * (DIFF_D ** -0.5 * LOG2E)
        lane = lax.broadcasted_iota(jnp.int32, (t_new, HEAD_W), 1)
        for h in range(N_HEADS):
            cols = slice(h * HEAD_W, (h + 1) * HEAD_W)
            r0 = h * rows_per_head
            qf[r0:r0 + t_new, :] = jnp.where(lane < DIFF_D, q[:, cols], 0.0)
            qf[r0 + t_new:r0 + 2 * t_new, :] = jnp.where(lane >= DIFF_D, q[:, cols], 0.0)
        qbf[...] = qf[...].astype(BF16)
        m_ref[...] = jnp.full(m_ref.shape, -jnp.inf, F32)
        acc_ref[...] = jnp.zeros(acc_ref.shape, F32)

    for g in range(n_pg):
        pg = slice(g * pg_rows, (g + 1) * pg_rows)
        v1all[pg, 0:HEAD_W] = v_refs[g][0].astype(BF16)
        s_ref[:, pg] = _dot_nt(qbf[...], k_refs[g][0].astype(BF16))

    def add_mask(r0, j):
        c0 = (j % pg_slabs) * HEAD_W
        return mask_ref[r0:r0 + SAMPLE_RB, c0:c0 + HEAD_W]

    def add_last(r0, j):
        if j < (n_pg - 1) * pg_slabs:
            return add_mask(r0, j)
        c0 = (j % pg_slabs) * HEAD_W
        return bias_last_ref[r0:r0 + SAMPLE_RB, c0:c0 + HEAD_W]

    @pl.when(p < last)
    def _():
        _softmax_tile(s_ref, p_ref, m_ref, alpha_ref, add_mask, rb=SAMPLE_RB, keep_in_regs=False)

    @pl.when(p == last)
    def _():
        _softmax_tile(s_ref, p_ref, m_ref, alpha_ref, add_last, rb=SAMPLE_RB, keep_in_regs=False)

    _accumulate(acc_ref, alpha_ref, p_ref, [v1all[g * pg_rows:(g + 1) * pg_rows, :] for g in range(n_pg)])

    @pl.when(p == last)
    def _():
        knew[0:n_new, :] = kn_ref[0].astype(BF16)
        v1new[0:n_new, 0:HEAD_W] = vn_ref[0].astype(BF16)
        sn_ref[...] = _dot_nt(qbf[...], knew[...])
        _softmax_tile(sn_ref, pn_ref, m_ref, alpha_ref,
                      lambda r0, j: bias_new_ref[r0:r0 + SAMPLE_RB, :], rb=SAMPLE_RB, keep_in_regs=True)
        _accumulate(acc_ref, alpha_ref, pn_ref, [v1new[...]])

        lam = _lam(lamp_ref, lam_init)
        for h in range(N_HEADS):
            cols = slice(h * HEAD_W, (h + 1) * HEAD_W)
            r0 = h * rows_per_head
            r1 = r0 + t_new
            o = (acc_ref[r0:r1, 0:HEAD_W] / acc_ref[r0:r1, HEAD_W:2 * HEAD_W]
                 - lam * (acc_ref[r1:r1 + t_new, 0:HEAD_W] / acc_ref[r1:r1 + t_new, HEAD_W:2 * HEAD_W]))
            o_ref[:, cols] = (_rms(o, gn_ref[:, cols]) * (1.0 - lam_init)).astype(o_ref.dtype)


def _diff_sample(proj_a, dk, dv, cache_k, cache_v, page_table, rel_bias, lamp, gn, *, t_new, lam_init, name):
    batch, n_pages = page_table.shape
    n_phys, page = cache_k.shape[0], cache_k.shape[1]
    past = n_pages * page
    rows = N_HEADS * 2 * t_new
    kv_rows = page * N_HEADS
    new_rows = 128
    assert t_new * N_HEADS <= new_rows
    q_head = jnp.repeat(jnp.arange(N_HEADS), 2 * t_new)
    q_pos = past + jnp.tile(jnp.arange(t_new), 2 * N_HEADS)

    def buckets(n_cols, first_key_pos, n_valid_cols):
        col = jnp.arange(n_cols)
        k_pos = first_key_pos + col // N_HEADS
        ok = ((col % N_HEADS)[None, :] == q_head[:, None]) & (k_pos[None, :] <= q_pos[:, None])
        ok = ok & (col < n_valid_cols)[None, :]
        return jnp.where(ok, _t5_bucket(q_pos, k_pos), -1).astype(jnp.int32)

    bkt_last = buckets(kv_rows, past - page, kv_rows)
    bkt_new = buckets(new_rows, past, t_new * N_HEADS)
    n_pg = math.gcd(n_pages, PAGES_PER_STEP)

    def page_spec(g):
        return pl.BlockSpec((1, kv_rows, HEAD_W), lambda b, p, pt, g=g: (pt[b, p * n_pg + g], 0, 0))

    grid_spec = pltpu.PrefetchScalarGridSpec(
        num_scalar_prefetch=1,
        grid=(batch, n_pages // n_pg),
        in_specs=([pl.BlockSpec((t_new, GROUP_W), lambda b, p, pt: (b, N_SECTIONS_A - 1))]
                  + [page_spec(g) for g in range(n_pg)] + [page_spec(g) for g in range(n_pg)]
                  + [pl.BlockSpec((1, t_new * N_HEADS, HEAD_W), lambda b, p, pt: (b, 0, 0)),
                     pl.BlockSpec((1, t_new * N_HEADS, HEAD_W), lambda b, p, pt: (b, 0, 0)),
                     pl.BlockSpec((rows, kv_rows), lambda b, p, pt: (0, 0)),
                     pl.BlockSpec((rows, new_rows), lambda b, p, pt: (0, 0)),
                     pl.BlockSpec(memory_space=pltpu.SMEM),
                     pl.BlockSpec((4, DIFF_D), lambda b, p, pt: (0, 0)),
                     pl.BlockSpec((1, GROUP_W), lambda b, p, pt: (0, 0))]),
        out_specs=pl.BlockSpec((t_new, GROUP_W), lambda b, p, pt: (b, 0)),
        scratch_shapes=[pltpu.VMEM((rows, HEAD_W), F32), pltpu.VMEM((rows, HEAD_W), BF16),
                        pltpu.VMEM((n_pg * kv_rows, 2 * HEAD_W), BF16),
                        pltpu.VMEM((new_rows, HEAD_W), BF16), pltpu.VMEM((new_rows, 2 * HEAD_W), BF16),
                        pltpu.VMEM((rows, n_pg * kv_rows), F32), pltpu.VMEM((rows, n_pg * kv_rows), BF16),
                        pltpu.VMEM((rows, new_rows), F32), pltpu.VMEM((rows, new_rows), BF16),
                        pltpu.VMEM((rows, HEAD_W), F32), pltpu.VMEM((rows, HEAD_W), F32),
                        pltpu.VMEM((rows, 2 * HEAD_W), F32),
                        pltpu.VMEM((rows, kv_rows), F32), pltpu.VMEM((rows, kv_rows), F32),
                        pltpu.VMEM((rows, new_rows), F32)],
    )
    ck = cache_k.reshape(n_phys, kv_rows, HEAD_W)
    cv = cache_v.reshape(n_phys, kv_rows, HEAD_W)
    return pl.pallas_call(
        functools.partial(_diff_sample_kernel, t_new=t_new, lam_init=lam_init, n_pg=n_pg),
        grid_spec=grid_spec,
        out_shape=jax.ShapeDtypeStruct((batch * t_new, GROUP_W), F32),
        compiler_params=_cparams(("arbitrary", "arbitrary")),
        name=name,
    )(page_table, proj_a, *([ck] * n_pg), *([cv] * n_pg),
      dk.reshape(batch, t_new * N_HEADS, HEAD_W), dv.reshape(batch, t_new * N_HEADS, HEAD_W),
      bkt_last, bkt_new, rel_bias, lamp, gn)


def _proj_norm_res_kernel(*refs, n_in):
    a_refs, w_refs = refs[:n_in], refs[n_in:2 * n_in]
    gain_ref, x_ref, o_ref = refs[2 * n_in:]
    y = _dot(a_refs[0][...].astype(BF16), w_refs[0][...])
    for a_ref, w_ref in zip(a_refs[1:], w_refs[1:]):
        y = y + _dot(a_ref[...].astype(BF16), w_ref[...])
    o_ref[...] = x_ref[...] + _rms(y, gain_ref[...])


def _proj_norm_res(a_list, w, gain, x, *, tm, name):
    m, n = x.shape
    n_in = len(a_list)
    ka = a_list[0].shape[1]
    return pl.pallas_call(
        functools.partial(_proj_norm_res_kernel, n_in=n_in),
        grid=(m // tm,),
        in_specs=([pl.BlockSpec((tm, ka), lambda i: (i, 0)) for _ in range(n_in)]
                  + [pl.BlockSpec((ka, n), lambda i, s=s: (s, 0)) for s in range(n_in)]
                  + [pl.BlockSpec((1, n), lambda i: (0, 0)), pl.BlockSpec((tm, n), lambda i: (i, 0))]),
        out_specs=pl.BlockSpec((tm, n), lambda i: (i, 0)),
        out_shape=jax.ShapeDtypeStruct((m, n), F32),
        compiler_params=_cparams(("parallel",)),
        name=name,
    )(*a_list, *([w] * n_in), gain, x)


def _cross_kernel(q_ref, mk_ref, mv_ref, o_ref):
    for h in range(N_XHEADS):
        cols = slice(h * HEAD_W, (h + 1) * HEAD_W)
        s = _dot_nt(q_ref[0, :, cols].astype(BF16), mk_ref[0, :, cols].astype(BF16)) * (HEAD_W ** -0.5)
        s = s - jnp.max(s, axis=-1, keepdims=True)
        e = jnp.exp(s)
        p = e / jnp.sum(e, axis=-1, keepdims=True)
        o_ref[0, :, cols] = _dot(p.astype(BF16), mv_ref[0, :, cols].astype(BF16)).astype(o_ref.dtype)


def _cross_attend(q, mk, mv, *, tq, out_dtype, name):
    b, t, w = q.shape
    n_mem = mk.shape[1]
    return pl.pallas_call(
        _cross_kernel,
        grid=(b, t // tq),
        in_specs=[pl.BlockSpec((1, tq, w), lambda i, j: (i, j, 0)),
                  pl.BlockSpec((1, n_mem, w), lambda i, j: (i, 0, 0)),
                  pl.BlockSpec((1, n_mem, w), lambda i, j: (i, 0, 0))],
        out_specs=pl.BlockSpec((1, tq, w), lambda i, j: (i, j, 0)),
        out_shape=jax.ShapeDtypeStruct((b, t, w), out_dtype),
        compiler_params=_cparams(("parallel", "parallel")),
        name=name,
    )(q, mk, mv)


def _ffn_kernel(x_ref, gpre_ref, wg_ref, wu_ref, wd_ref, gpost_ref, o_ref, h_ref, acc_ref):
    f = pl.program_id(1)

    @pl.when(f == 0)
    def _():
        h_ref[...] = _rms(x_ref[...], gpre_ref[...]).astype(BF16)
        acc_ref[...] = jnp.zeros(acc_ref.shape, F32)

    h = h_ref[...]
    g = _dot(h, wg_ref[...])
    a = (g * jax.nn.sigmoid(g)) * _dot(h, wu_ref[...])
    acc_ref[...] += _dot(a.astype(BF16), wd_ref[...])

    @pl.when(f == pl.num_programs(1) - 1)
    def _():
        o_ref[...] = x_ref[...] + _rms(acc_ref[...], gpost_ref[...])


def _ffn(x, gpre, wg, wu, wd, gpost, *, tm, tf, name):
    m, d = x.shape
    dff = wg.shape[1]
    return pl.pallas_call(
        _ffn_kernel,
        grid=(m // tm, dff // tf),
        in_specs=[pl.BlockSpec((tm, d), lambda i, f: (i, 0)),
                  pl.BlockSpec((1, d), lambda i, f: (0, 0)),
                  pl.BlockSpec((d, tf), lambda i, f: (0, f)),
                  pl.BlockSpec((d, tf), lambda i, f: (0, f)),
                  pl.BlockSpec((tf, d), lambda i, f: (f, 0)),
                  pl.BlockSpec((1, d), lambda i, f: (0, 0))],
        out_specs=pl.BlockSpec((tm, d), lambda i, f: (i, 0)),
        out_shape=jax.ShapeDtypeStruct((m, d), F32),
        scratch_shapes=[pltpu.VMEM((tm, d), BF16), pltpu.VMEM((tm, d), F32)],
        compiler_params=_cparams(("parallel", "arbitrary")),
        name=name,
    )(x, gpre, wg, wu, wd, gpost)


def _rope_tables(pos):
    inv_freq = ROPE_BASE ** (-jnp.arange(0, HEAD_W, 2, dtype=F32) / HEAD_W)
    ang = pos.astype(F32)[:, None] * inv_freq[None, :]
    cos = jnp.repeat(jnp.cos(ang), 2, axis=-1)
    sin = jnp.sin(ang)
    return cos, jnp.stack([-sin, sin], axis=-1).reshape(cos.shape)


def _layer(x, pos, s0, mem_k, mem_v, paged, w, *, batch, seq, lam_init, tag):
    m = batch * seq
    big = paged is None
    tm = 512 if big else m
    proj_a, dk, dv = _in_proj(x, w["n_pre_mix"], w["w_in"], tm=1024 if big else m, tn=512, name=f"in_proj_{tag}")
    cos, sin = _rope_tables(pos)
    ret_o, ret_s = _retention(proj_a, cos, sin, w["lg"], w["ret_gn"], s0, batch=batch, seq=seq,
                              heads=1 if big else N_HEADS, out_dtype=BF16 if big else F32, name=f"retention_{tag}")
    if big:
        diff_o = _diff_prompt(proj_a, dk, dv, w["rel_bias"], w["lamp"], w["diff_gn"], batch=batch, seq=seq,
                              t=512, lam_init=lam_init, name=f"diff_attn_{tag}")
    else:
        diff_o = _diff_sample(proj_a, dk, dv, *paged, w["rel_bias"], w["lamp"], w["diff_gn"], t_new=seq,
                              lam_init=lam_init, name=f"diff_attn_{tag}")
    x = _proj_norm_res([ret_o, diff_o], w["w_out"], w["n_post_mix"], x, tm=tm, name=f"out_proj_{tag}")
    xw = N_XHEADS * HEAD_W
    q = _norm_matmul(x, w["n_pre_x"], w["w_xq"], tm=tm, tn=xw, out_dtype=BF16 if big else F32, name=f"xq_{tag}")
    o = _cross_attend(q.reshape(batch, seq, xw), mem_k, mem_v, tq=512 if big else seq,
                      out_dtype=BF16 if big else F32, name=f"cross_{tag}")
    x = _proj_norm_res([o.reshape(m, xw)], w["w_xo"], w["n_post_x"], x, tm=tm, name=f"xo_{tag}")
    x = _ffn(x, w["n_pre_ffn"], w["w_gate"], w["w_up"], w["w_down"], w["n_post_ffn"], tm=tm, tf=512,
             name=f"ffn_{tag}")
    return x, dk, dv, ret_s


def kernel(x_prompt, x_sample, cache_k, cache_v, state_ret, cache_mem_k, cache_mem_v, page_table, mem_prompt,
           rel_bias, norm_pre_mix, norm_post_mix, norm_pre_x, norm_post_x, norm_pre_ffn, norm_post_ffn, norm_mem,
           w_in, w_out, ret_gn, diff_gn, lam_q1, lam_k1, lam_q2, lam_k2,
           w_xq, w_xk, w_xv, w_xo, w_gate, w_up, w_down):
    b_p, t_p, d = x_prompt.shape
    b_s, t_s, _ = x_sample.shape
    depth = w_in.shape[0]
    n_mem = mem_prompt.shape[1]
    xw = N_XHEADS * HEAD_W
    past_len = page_table.shape[1] * cache_k.shape[2]
    pos_p = jnp.arange(t_p)
    pos_s = past_len + jnp.arange(t_s)
    log_g = jnp.log1p(-(2.0 ** (-5.0 - jnp.arange(N_HEADS, dtype=F32))))
    lg = jnp.broadcast_to(log_g[:, None, None], (N_HEADS, 8, HEAD_W))
    xp = x_prompt.reshape(b_p * t_p, d)
    xs = x_sample.reshape(b_s * t_s, d)
    outs = [[] for _ in range(8)]
    for l in range(depth):
        lam_init = 0.8 - 0.6 * math.exp(-0.3 * l)
        w = {
            "n_pre_mix": norm_pre_mix[l][None], "n_post_mix": norm_post_mix[l][None],
            "n_pre_x": norm_pre_x[l][None], "n_post_x": norm_post_x[l][None],
            "n_pre_ffn": norm_pre_ffn[l][None], "n_post_ffn": norm_post_ffn[l][None],
            "w_in": w_in[l].astype(BF16), "w_out": w_out[l].astype(BF16),
            "ret_gn": ret_gn[l][None], "diff_gn": diff_gn[l][None],
            "lamp": jnp.stack([lam_q1[l], lam_k1[l], lam_q2[l], lam_k2[l]]),
            "w_xq": w_xq[l].astype(BF16), "w_xo": w_xo[l].astype(BF16),
            "w_gate": w_gate[l].astype(BF16), "w_up": w_up[l].astype(BF16), "w_down": w_down[l].astype(BF16),
            "rel_bias": rel_bias, "lg": lg,
        }
        mem_flat = mem_prompt.reshape(b_p * n_mem, d)
        mk_p = _norm_matmul(mem_flat, norm_mem[l][None], w_xk[l].astype(BF16), tm=b_p * n_mem, tn=xw,
                            out_dtype=F32, name="mem_k")
        mv_p = _norm_matmul(mem_flat, norm_mem[l][None], w_xv[l].astype(BF16), tm=b_p * n_mem, tn=xw,
                            out_dtype=F32, name="mem_v")
        s0 = jnp.zeros((b_p, N_HEADS, HEAD_W, HEAD_W), F32)
        xp, dk_p, dv_p, s_p = _layer(xp, pos_p, s0, mk_p.reshape(b_p, n_mem, xw), mv_p.reshape(b_p, n_mem, xw),
                                     None, w, batch=b_p, seq=t_p, lam_init=lam_init, tag="prompt")
        xs, dk_s, dv_s, s_s = _layer(xs, pos_s, state_ret[l], cache_mem_k[l].reshape(b_s, n_mem, xw),
                                     cache_mem_v[l].reshape(b_s, n_mem, xw),
                                     (cache_k[l], cache_v[l], page_table), w,
                                     batch=b_s, seq=t_s, lam_init=lam_init, tag="sample")
        per_layer = (dk_p.reshape(b_p, t_p, N_HEADS, HEAD_W), dv_p.reshape(b_p, t_p, N_HEADS, HEAD_W), s_p,
                     mk_p.reshape(b_p, n_mem, N_XHEADS, HEAD_W), mv_p.reshape(b_p, n_mem, N_XHEADS, HEAD_W),
                     dk_s.reshape(b_s, t_s, N_HEADS, HEAD_W), dv_s.reshape(b_s, t_s, N_HEADS, HEAD_W), s_s)
        for acc, val in zip(outs, per_layer):
            acc.append(val)
    stacked = [jnp.stack(o) for o in outs]
    return (xp.reshape(b_p, t_p, d), xs.reshape(b_s, t_s, d), *stacked)
```

```python
import functools
import math

import jax
import jax.numpy as jnp
from jax import lax
from jax.experimental import pallas as pl
from jax.experimental.pallas import tpu as pltpu

F32 = jnp.float32
BF16 = jnp.bfloat16

D_MODEL = 2048
N_HEADS = 8
HEAD_W = 128
GROUP_W = N_HEADS * HEAD_W
DIFF_D = 64
ROPE_BASE = 10000.0
RET_CHUNK = 128
N_BUCKETS = 32
MAX_EXACT = N_BUCKETS // 2
MAX_DISTANCE = 128
N_XHEADS = 4
RMS_EPS = 1e-6
NEG_BIG = -1e30
LOG2E = math.log2(math.e)
PROMPT_RB = 32
SAMPLE_RB = 16
PAGES_PER_STEP = 8
RET_UNROLL = 8
N_SECTIONS_A = 5
VMEM_LIMIT = 56 * 1024 * 1024


def _cparams(sem):
    return pltpu.CompilerParams(dimension_semantics=sem, vmem_limit_bytes=VMEM_LIMIT)


def _rms(x, gain):
    return x * lax.rsqrt(jnp.mean(x * x, axis=-1, keepdims=True) + RMS_EPS) * gain


def _dot(a, b):
    return jnp.dot(a, b, preferred_element_type=F32)


def _dot_nt(a, b):
    return lax.dot_general(a, b, (((1,), (1,)), ((), ())), preferred_element_type=F32)


def _dot_tn(a, b):
    return lax.dot_general(a, b, (((0,), (0,)), ((), ())), preferred_element_type=F32)


def _norm_matmul_kernel(x_ref, g_ref, w_ref, o_ref, h_ref):
    @pl.when(pl.program_id(1) == 0)
    def _():
        h_ref[...] = _rms(x_ref[...], g_ref[...]).astype(BF16)

    o_ref[...] = _dot(h_ref[...], w_ref[...]).astype(o_ref.dtype)


def _norm_matmul(x, gain, w, *, tm, tn, out_dtype, name):
    m, k = x.shape
    n = w.shape[1]
    return pl.pallas_call(
        _norm_matmul_kernel,
        grid=(m // tm, n // tn),
        in_specs=[pl.BlockSpec((tm, k), lambda i, j: (i, 0)),
                  pl.BlockSpec((1, k), lambda i, j: (0, 0)),
                  pl.BlockSpec((k, tn), lambda i, j: (0, j))],
        out_specs=pl.BlockSpec((tm, tn), lambda i, j: (i, j)),
        out_shape=jax.ShapeDtypeStruct((m, n), out_dtype),
        scratch_shapes=[pltpu.VMEM((tm, k), BF16)],
        compiler_params=_cparams(("parallel", "arbitrary")),
        name=name,
    )(x, gain, w)


def _in_proj_kernel(x_ref, g_ref, w_ref, a_ref, dk_ref, dv_ref, h_ref, *, n_a, n_g):
    j = pl.program_id(1)

    @pl.when(j == 0)
    def _():
        h_ref[...] = _rms(x_ref[...], g_ref[...]).astype(BF16)

    res = _dot(h_ref[...], w_ref[...])

    @pl.when(j < n_a)
    def _():
        a_ref[...] = res

    @pl.when(jnp.logical_and(j >= n_a, j < n_a + n_g))
    def _():
        dk_ref[...] = res

    @pl.when(j >= n_a + n_g)
    def _():
        dv_ref[...] = res


def _in_proj(x, gain, w, *, tm, tn, name):
    m, k = x.shape
    n = w.shape[1]
    n_g = GROUP_W // tn
    n_a = N_SECTIONS_A * n_g
    return pl.pallas_call(
        functools.partial(_in_proj_kernel, n_a=n_a, n_g=n_g),
        grid=(m // tm, n // tn),
        in_specs=[pl.BlockSpec((tm, k), lambda i, j: (i, 0)),
                  pl.BlockSpec((1, k), lambda i, j: (0, 0)),
                  pl.BlockSpec((k, tn), lambda i, j: (0, j))],
        out_specs=[pl.BlockSpec((tm, tn), lambda i, j: (i, jnp.minimum(j, n_a - 1))),
                   pl.BlockSpec((tm, tn), lambda i, j: (i, jnp.clip(j - n_a, 0, n_g - 1))),
                   pl.BlockSpec((tm, tn), lambda i, j: (i, jnp.clip(j - n_a - n_g, 0, n_g - 1)))],
        out_shape=[jax.ShapeDtypeStruct((m, N_SECTIONS_A * GROUP_W), F32),
                   jax.ShapeDtypeStruct((m, GROUP_W), F32),
                   jax.ShapeDtypeStruct((m, GROUP_W), F32)],
        scratch_shapes=[pltpu.VMEM((tm, k), BF16)],
        compiler_params=_cparams(("parallel", "arbitrary")),
        name=name,
    )(x, gain, w)


def _rotate(x, cos, sin_signed):
    lane = lax.broadcasted_iota(jnp.int32, x.shape, 1)
    nxt = pltpu.roll(x, HEAD_W - 1, 1)
    prv = pltpu.roll(x, 1, 1)
    return x * cos + jnp.where(lane % 2 == 0, nxt, prv) * sin_signed


def _retention_kernel(q_ref, k_ref, v_ref, g_ref, cos_ref, sin_ref, lg_ref, gn_ref, s0_ref,
                      o_ref, sf_ref, *, chunk, n_chunks, heads):
    rows_mm = RET_CHUNK
    ri = lax.broadcasted_iota(jnp.int32, (rows_mm, rows_mm), 0)
    ci = lax.broadcasted_iota(jnp.int32, (rows_mm, rows_mm), 1)
    rel = (ri - ci).astype(F32)
    idx = lax.broadcasted_iota(jnp.int32, (rows_mm, 1), 0).astype(F32)

    def pad(a):
        if chunk == rows_mm:
            return a
        return jnp.concatenate([a, jnp.zeros((rows_mm - chunk, a.shape[1]), a.dtype)], axis=0)

    for hh in range(heads):
        cols = slice(hh * HEAD_W, (hh + 1) * HEAD_W)
        lg = lg_ref[hh, 0:1, 0:1]
        intra = jnp.where(rel >= 0, jnp.exp(lg * jnp.maximum(rel, 0.0)), 0.0)
        q_dec = jnp.exp(lg * (idx + 1.0))
        k_dec = jnp.exp(lg * jnp.maximum(chunk - 1.0 - idx, 0.0))
        chunk_dec = jnp.exp(lg * float(chunk))
        gn = gn_ref[:, cols]

        def body(c, s, cols=cols, intra=intra, q_dec=q_dec, k_dec=k_dec, chunk_dec=chunk_dec, gn=gn):
            if n_chunks == 1:
                rows = slice(0, chunk)
            else:
                rows = pl.ds(pl.multiple_of(c * chunk, chunk), chunk)
            cos = cos_ref[rows, :]
            sin = sin_ref[rows, :]
            q = pad(_rotate(q_ref[rows, cols], cos, sin))
            k = pad(_rotate(k_ref[rows, cols], cos, sin) * (HEAD_W ** -0.5))
            qb = q.astype(BF16)
            vb = pad(v_ref[rows, cols]).astype(BF16)
            scores = _dot_nt(qb, k.astype(BF16)) * intra
            o = _dot(scores.astype(BF16), vb) + _dot(qb, s.astype(BF16)) * q_dec
            s_new = s * chunk_dec + _dot_tn((k * k_dec).astype(BF16), vb)
            gate = g_ref[rows, cols]
            y = _rms(o[0:chunk, :], gn)
            o_ref[rows, cols] = ((gate * jax.nn.sigmoid(gate)) * y).astype(o_ref.dtype)
            return s_new

        s0 = s0_ref[0, hh]
        if n_chunks == 1:
            s_fin = body(0, s0)
        else:
            s_fin = lax.fori_loop(0, n_chunks, body, s0, unroll=math.gcd(n_chunks, RET_UNROLL))
        sf_ref[0, hh] = s_fin


def _retention(proj_a, cos, sin, lg, gn, s0, *, batch, seq, heads, out_dtype, name):
    chunk = math.gcd(seq, RET_CHUNK)
    w = heads * HEAD_W
    n_hb = N_HEADS // heads

    def sec(s):
        return pl.BlockSpec((seq, w), lambda b, h, s=s: (b, s * n_hb + h))

    return pl.pallas_call(
        functools.partial(_retention_kernel, chunk=chunk, n_chunks=seq // chunk, heads=heads),
        grid=(batch, n_hb),
        in_specs=[sec(0), sec(1), sec(2), sec(3),
                  pl.BlockSpec((seq, HEAD_W), lambda b, h: (0, 0)),
                  pl.BlockSpec((seq, HEAD_W), lambda b, h: (0, 0)),
                  pl.BlockSpec((heads, 8, HEAD_W), lambda b, h: (h, 0, 0)),
                  pl.BlockSpec((1, w), lambda b, h: (0, h)),
                  pl.BlockSpec((1, heads, HEAD_W, HEAD_W), lambda b, h: (b, h, 0, 0))],
        out_specs=[pl.BlockSpec((seq, w), lambda b, h: (b, h)),
                   pl.BlockSpec((1, heads, HEAD_W, HEAD_W), lambda b, h: (b, h, 0, 0))],
        out_shape=[jax.ShapeDtypeStruct((batch * seq, GROUP_W), out_dtype),
                   jax.ShapeDtypeStruct((batch, N_HEADS, HEAD_W, HEAD_W), F32)],
        compiler_params=_cparams(("parallel", "parallel")),
        name=name,
    )(proj_a, proj_a, proj_a, proj_a, cos, sin, lg, gn, s0)


def _t5_bucket(qpos, kpos):
    n = jnp.maximum(qpos[:, None] - kpos[None, :], 0)
    nf = jnp.maximum(n, 1).astype(F32)
    large = MAX_EXACT + (jnp.log(nf / MAX_EXACT) / math.log(MAX_DISTANCE / MAX_EXACT)
                         * (N_BUCKETS - MAX_EXACT)).astype(jnp.int32)
    large = jnp.minimum(large, N_BUCKETS - 1)
    return jnp.where(n < MAX_EXACT, n, large)


def _bias_from_buckets(bkt, rb_ref, h):
    far = rb_ref[N_BUCKETS - 1, h]
    b = jnp.zeros(bkt.shape, F32)
    for u in range(N_BUCKETS - 1):
        b = jnp.where(bkt == u, rb_ref[u, h] - far, b)
    return b


def _lam(lamp_ref, lam_init):
    e1 = jnp.exp(jnp.sum(lamp_ref[0:1, :] * lamp_ref[1:2, :], axis=-1, keepdims=True))
    e2 = jnp.exp(jnp.sum(lamp_ref[2:3, :] * lamp_ref[3:4, :], axis=-1, keepdims=True))
    return e1 - e2 + lam_init


def _softmax_tile(s_ref, p_ref, m_ref, alpha_ref, add_fn, *, rb, keep_in_regs, cols=None):
    n_rows = s_ref.shape[0]
    col0, n_cols = (0, s_ref.shape[1]) if cols is None else cols
    n_slabs = n_cols // HEAD_W
    for r0 in range(0, n_rows, rb):
        r = slice(r0, r0 + rb)
        m_prev = m_ref[r, :]
        vals, part = [], None
        for j in range(n_slabs):
            c = slice(col0 + j * HEAD_W, col0 + (j + 1) * HEAD_W)
            s = s_ref[r, c]
            if add_fn is not None:
                s = s + add_fn(r0, j)
                if not keep_in_regs:
                    s_ref[r, c] = s
            if keep_in_regs:
                vals.append(s)
            part = s if part is None else jnp.maximum(part, s)
        m_new = jnp.maximum(m_prev, jnp.max(part, axis=-1, keepdims=True))
        for j in range(n_slabs):
            c = slice(col0 + j * HEAD_W, col0 + (j + 1) * HEAD_W)
            s = vals[j] if keep_in_regs else s_ref[r, c]
            p_ref[r, c] = jnp.exp2(s - m_new).astype(BF16)
        alpha_ref[r, :] = jnp.exp2(m_prev - m_new)
        m_ref[r, :] = m_new


def _accumulate(acc_ref, alpha_ref, p, v1):
    a = alpha_ref[...]
    acc_ref[...] = jnp.concatenate([a, a], axis=1) * acc_ref[...] + _dot(p, v1)


def _diff_prompt_kernel(q_ref, k_ref, v_ref, bkt_ref, rb_ref, lamp_ref, gn_ref, o_ref,
                        kbf, v1, qs, s_a, s_b, p_a, p_b, m_ref, alpha_a, alpha_b, acc_ref, bias_ref,
                        *, t, lam_init):
    h = pl.program_id(1)
    qi = pl.program_id(2)

    @pl.when(qi == 0)
    def _():
        kbf[...] = k_ref[...].astype(BF16)
        v1[:, 0:HEAD_W] = v_ref[...].astype(BF16)
        v1[:, HEAD_W:2 * HEAD_W] = jnp.ones((v1.shape[0], HEAD_W), BF16)
        ri = lax.broadcasted_iota(jnp.int32, (t, t), 0)
        ci = lax.broadcasted_iota(jnp.int32, (t, t), 1)
        bias_ref[0] = jnp.where(ci <= ri, _bias_from_buckets(bkt_ref[0], rb_ref, h) * LOG2E, NEG_BIG)
        bias_ref[1] = _bias_from_buckets(bkt_ref[1], rb_ref, h) * LOG2E
        bias_ref[2] = jnp.zeros((t, t), F32)

    q = q_ref[...] * (DIFF_D ** -0.5 * LOG2E)
    lane = lax.broadcasted_iota(jnp.int32, q.shape, 1)
    qs[0:t, :] = jnp.where(lane < DIFF_D, q, 0.0).astype(BF16)
    qs[t:2 * t, :] = jnp.where(lane >= DIFF_D, q, 0.0).astype(BF16)
    m_ref[...] = jnp.full(m_ref.shape, -jnp.inf, F32)
    acc_ref[...] = jnp.zeros(acc_ref.shape, F32)

    def scores(k, s_buf):
        rows = pl.ds(pl.multiple_of(jnp.minimum(k, qi) * t, t), t)
        s_buf[...] = _dot_nt(qs[...], kbf[rows, :])

    def update(k, s_buf, p_buf, alpha_buf):
        which = jnp.where(k == qi, 0, jnp.where(k == qi - 1, 1, 2))

        def add_fn(r0, j):
            q0 = r0 % t
            return bias_ref[which, q0:q0 + PROMPT_RB, j * HEAD_W:(j + 1) * HEAD_W]

        _softmax_tile(s_buf, p_buf, m_ref, alpha_buf, add_fn, rb=PROMPT_RB, keep_in_regs=True)
        rows = pl.ds(pl.multiple_of(k * t, t), t)
        _accumulate(acc_ref, alpha_buf, p_buf[...], v1[rows, :])

    n_tiles = qi + 1
    scores(0, s_a)

    def pair(j, carry):
        k0 = 2 * j
        scores(k0 + 1, s_b)
        update(k0, s_a, p_a, alpha_a)
        scores(k0 + 2, s_a)
        update(k0 + 1, s_b, p_b, alpha_b)
        return carry

    lax.fori_loop(0, n_tiles // 2, pair, 0)

    @pl.when(n_tiles % 2 == 1)
    def _():
        update(qi, s_a, p_a, alpha_a)

    lam = _lam(lamp_ref, lam_init)
    o = (acc_ref[0:t, 0:HEAD_W] / acc_ref[0:t, HEAD_W:2 * HEAD_W]
         - lam * (acc_ref[t:2 * t, 0:HEAD_W] / acc_ref[t:2 * t, HEAD_W:2 * HEAD_W]))
    o_ref[...] = (_rms(o, gn_ref[...]) * (1.0 - lam_init)).astype(o_ref.dtype)


def _diff_prompt(proj_a, dk, dv, rel_bias, lamp, gn, *, batch, seq, t, lam_init, name):
    nq = seq // t
    pos = jnp.arange(t)
    bkt = jnp.stack([_t5_bucket(pos, pos), _t5_bucket(pos + t, pos)]).astype(jnp.int32)
    q_sec = (N_SECTIONS_A - 1) * N_HEADS
    return pl.pallas_call(
        functools.partial(_diff_prompt_kernel, t=t, lam_init=lam_init),
        grid=(batch, N_HEADS, nq),
        in_specs=[pl.BlockSpec((t, HEAD_W), lambda b, h, i: (b * nq + i, q_sec + h)),
                  pl.BlockSpec((seq, HEAD_W), lambda b, h, i: (b, h)),
                  pl.BlockSpec((seq, HEAD_W), lambda b, h, i: (b, h)),
                  pl.BlockSpec((2, t, t), lambda b, h, i: (0, 0, 0)),
                  pl.BlockSpec(memory_space=pltpu.SMEM),
                  pl.BlockSpec((4, DIFF_D), lambda b, h, i: (0, 0)),
                  pl.BlockSpec((1, HEAD_W), lambda b, h, i: (0, h))],
        out_specs=pl.BlockSpec((t, HEAD_W), lambda b, h, i: (b * nq + i, h)),
        out_shape=jax.ShapeDtypeStruct((batch * seq, GROUP_W), BF16),
        scratch_shapes=[pltpu.VMEM((seq, HEAD_W), BF16), pltpu.VMEM((seq, 2 * HEAD_W), BF16),
                        pltpu.VMEM((2 * t, HEAD_W), BF16),
                        pltpu.VMEM((2 * t, t), F32), pltpu.VMEM((2 * t, t), F32),
                        pltpu.VMEM((2 * t, t), BF16), pltpu.VMEM((2 * t, t), BF16),
                        pltpu.VMEM((2 * t, HEAD_W), F32), pltpu.VMEM((2 * t, HEAD_W), F32),
                        pltpu.VMEM((2 * t, HEAD_W), F32), pltpu.VMEM((2 * t, 2 * HEAD_W), F32),
                        pltpu.VMEM((3, t, t), F32)],
        compiler_params=_cparams(("parallel", "parallel", "arbitrary")),
        name=name,
    )(proj_a, dk, dv, bkt, rel_bias, lamp, gn)


def _diff_sample_kernel(pt_ref, q_ref, *refs, t_new, lam_init, n_pg):
    del pt_ref
    k_refs, v_refs = refs[:n_pg], refs[n_pg:2 * n_pg]
    (kn_ref, vn_ref, bkt_last_ref, bkt_new_ref, rb_ref, lamp_ref, gn_ref, o_ref,
     qf, qbf, v1all, knew, v1new, s_ref, p_ref, sn_ref, pn_ref, m_ref, alpha_ref, acc_ref,
     mask_ref, bias_last_ref, bias_new_ref) = refs[2 * n_pg:]
    b = pl.program_id(0)
    p = pl.program_id(1)
    last = pl.num_programs(1) - 1
    rows_per_head = 2 * t_new
    n_new = t_new * N_HEADS
    pg_rows = mask_ref.shape[1]

    def per_head_bias(bkt_ref):
        parts = []
        for h in range(N_HEADS):
            r0 = h * rows_per_head
            bk = bkt_ref[r0:r0 + rows_per_head, :]
            parts.append(jnp.where(bk >= 0, _bias_from_buckets(bk, rb_ref, h) * LOG2E, NEG_BIG))
        return jnp.concatenate(parts, axis=0)

    @pl.when(jnp.logical_and(b == 0, p == 0))
    def _():
        mask_ref[...] = jnp.where(bkt_last_ref[...] >= 0, 0.0, NEG_BIG)
        bias_last_ref[...] = per_head_bias(bkt_last_ref)
        bias_new_ref[...] = per_head_bias(bkt_new_ref)
        knew[...] = jnp.zeros(knew.shape, BF16)
        v1new[:, 0:HEAD_W] = jnp.zeros((v1new.shape[0], HEAD_W), BF16)
        v1new[:, HEAD_W:2 * HEAD_W] = jnp.ones((v1new.shape[0], HEAD_W), BF16)
        v1all[:, HEAD_W:2 * HEAD_W] = jnp.ones((v1all.shape[0], HEAD_W), BF16)

    @pl.when(p == 0)
    def _():
        q = q_ref[...] * (DIFF_D ** -0.5 * LOG2E)
        lane = lax.broadcasted_iota(jnp.int32, (t_new, HEAD_W), 1)
        for h in range(N_HEADS):
            cols = slice(h * HEAD_W, (h + 1) * HEAD_W)
            r0 = h * rows_per_head
            qf[r0:r0 + t_new, :] = jnp.where(lane < DIFF_D, q[:, cols], 0.0)
            qf[r0 + t_new:r0 + 2 * t_new, :] = jnp.where(lane >= DIFF_D, q[:, cols], 0.0)
        qbf[...] = qf[...].astype(BF16)
        m_ref[...] = jnp.full(m_ref.shape, -jnp.inf, F32)
        acc_ref[...] = jnp.zeros(acc_ref.shape, F32)

    def add_mask(r0, j):
        return mask_ref[r0:r0 + SAMPLE_RB, j * HEAD_W:(j + 1) * HEAD_W]

    def add_last(r0, j):
        return bias_last_ref[r0:r0 + SAMPLE_RB, j * HEAD_W:(j + 1) * HEAD_W]

    def page_update(g, add_fn):
        pg = slice(g * pg_rows, (g + 1) * pg_rows)
        _softmax_tile(s_ref, p_ref, m_ref, alpha_ref.at[g], add_fn, rb=SAMPLE_RB, keep_in_regs=False,
                      cols=(g * pg_rows, pg_rows))
        _accumulate(acc_ref, alpha_ref.at[g], p_ref[:, pg], v1all[pg, :])

    def page_scores(g):
        pg = slice(g * pg_rows, (g + 1) * pg_rows)
        v1all[pg, 0:HEAD_W] = v_refs[g][0].astype(BF16)
        s_ref[:, pg] = _dot_nt(qbf[...], k_refs[g][0].astype(BF16))

    page_scores(0)
    for g in range(n_pg - 1):
        page_scores(g + 1)
        page_update(g, add_mask)

    @pl.when(p < last)
    def _():
        page_update(n_pg - 1, add_mask)

    @pl.when(p == last)
    def _():
        page_update(n_pg - 1, add_last)
        knew[0:n_new, :] = kn_ref[0].astype(BF16)
        v1new[0:n_new, 0:HEAD_W] = vn_ref[0].astype(BF16)
        sn_ref[...] = _dot_nt(qbf[...], knew[...])
        _softmax_tile(sn_ref, pn_ref, m_ref, alpha_ref.at[0],
                      lambda r0, j: bias_new_ref[r0:r0 + SAMPLE_RB, :], rb=SAMPLE_RB, keep_in_regs=True)
        _accumulate(acc_ref, alpha_ref.at[0], pn_ref[...], v1new[...])

        lam = _lam(lamp_ref, lam_init)
        for h in range(N_HEADS):
            cols = slice(h * HEAD_W, (h + 1) * HEAD_W)
            r0 = h * rows_per_head
            r1 = r0 + t_new
            o = (acc_ref[r0:r1, 0:HEAD_W] / acc_ref[r0:r1, HEAD_W:2 * HEAD_W]
                 - lam * (acc_ref[r1:r1 + t_new, 0:HEAD_W] / acc_ref[r1:r1 + t_new, HEAD_W:2 * HEAD_W]))
            o_ref[:, cols] = (_rms(o, gn_ref[:, cols]) * (1.0 - lam_init)).astype(o_ref.dtype)


def _diff_sample(proj_a, dk, dv, cache_k, cache_v, page_table, rel_bias, lamp, gn, *, t_new, lam_init, name):
    batch, n_pages = page_table.shape
    n_phys, page = cache_k.shape[0], cache_k.shape[1]
    past = n_pages * page
    rows = N_HEADS * 2 * t_new
    kv_rows = page * N_HEADS
    new_rows = 128
    assert t_new * N_HEADS <= new_rows
    q_head = jnp.repeat(jnp.arange(N_HEADS), 2 * t_new)
    q_pos = past + jnp.tile(jnp.arange(t_new), 2 * N_HEADS)

    def buckets(n_cols, first_key_pos, n_valid_cols):
        col = jnp.arange(n_cols)
        k_pos = first_key_pos + col // N_HEADS
        ok = ((col % N_HEADS)[None, :] == q_head[:, None]) & (k_pos[None, :] <= q_pos[:, None])
        ok = ok & (col < n_valid_cols)[None, :]
        return jnp.where(ok, _t5_bucket(q_pos, k_pos), -1).astype(jnp.int32)

    bkt_last = buckets(kv_rows, past - page, kv_rows)
    bkt_new = buckets(new_rows, past, t_new * N_HEADS)
    n_pg = math.gcd(n_pages, PAGES_PER_STEP)

    def page_spec(g):
        return pl.BlockSpec((1, kv_rows, HEAD_W), lambda b, p, pt, g=g: (pt[b, p * n_pg + g], 0, 0))

    grid_spec = pltpu.PrefetchScalarGridSpec(
        num_scalar_prefetch=1,
        grid=(batch, n_pages // n_pg),
        in_specs=([pl.BlockSpec((t_new, GROUP_W), lambda b, p, pt: (b, N_SECTIONS_A - 1))]
                  + [page_spec(g) for g in range(n_pg)] + [page_spec(g) for g in range(n_pg)]
                  + [pl.BlockSpec((1, t_new * N_HEADS, HEAD_W), lambda b, p, pt: (b, 0, 0)),
                     pl.BlockSpec((1, t_new * N_HEADS, HEAD_W), lambda b, p, pt: (b, 0, 0)),
                     pl.BlockSpec((rows, kv_rows), lambda b, p, pt: (0, 0)),
                     pl.BlockSpec((rows, new_rows), lambda b, p, pt: (0, 0)),
                     pl.BlockSpec(memory_space=pltpu.SMEM),
                     pl.BlockSpec((4, DIFF_D), lambda b, p, pt: (0, 0)),
                     pl.BlockSpec((1, GROUP_W), lambda b, p, pt: (0, 0))]),
        out_specs=pl.BlockSpec((t_new, GROUP_W), lambda b, p, pt: (b, 0)),
        scratch_shapes=[pltpu.VMEM((rows, HEAD_W), F32), pltpu.VMEM((rows, HEAD_W), BF16),
                        pltpu.VMEM((n_pg * kv_rows, 2 * HEAD_W), BF16),
                        pltpu.VMEM((new_rows, HEAD_W), BF16), pltpu.VMEM((new_rows, 2 * HEAD_W), BF16),
                        pltpu.VMEM((rows, n_pg * kv_rows), F32), pltpu.VMEM((rows, n_pg * kv_rows), BF16),
                        pltpu.VMEM((rows, new_rows), F32), pltpu.VMEM((rows, new_rows), BF16),
                        pltpu.VMEM((rows, HEAD_W), F32), pltpu.VMEM((n_pg, rows, HEAD_W), F32),
                        pltpu.VMEM((rows, 2 * HEAD_W), F32),
                        pltpu.VMEM((rows, kv_rows), F32), pltpu.VMEM((rows, kv_rows), F32),
                        pltpu.VMEM((rows, new_rows), F32)],
    )
    ck = cache_k.reshape(n_phys, kv_rows, HEAD_W)
    cv = cache_v.reshape(n_phys, kv_rows, HEAD_W)
    return pl.pallas_call(
        functools.partial(_diff_sample_kernel, t_new=t_new, lam_init=lam_init, n_pg=n_pg),
        grid_spec=grid_spec,
        out_shape=jax.ShapeDtypeStruct((batch * t_new, GROUP_W), F32),
        compiler_params=_cparams(("arbitrary", "arbitrary")),
        name=name,
    )(page_table, proj_a, *([ck] * n_pg), *([cv] * n_pg),
      dk.reshape(batch, t_new * N_HEADS, HEAD_W), dv.reshape(batch, t_new * N_HEADS, HEAD_W),
      bkt_last, bkt_new, rel_bias, lamp, gn)


def _proj_norm_res_kernel(*refs, n_in):
    a_refs, w_refs = refs[:n_in], refs[n_in:2 * n_in]
    gain_ref, x_ref, o_ref = refs[2 * n_in:]
    y = _dot(a_refs[0][...].astype(BF16), w_refs[0][...])
    for a_ref, w_ref in zip(a_refs[1:], w_refs[1:]):
        y = y + _dot(a_ref[...].astype(BF16), w_ref[...])
    o_ref[...] = x_ref[...] + _rms(y, gain_ref[...])


def _proj_norm_res(a_list, w, gain, x, *, tm, name):
    m, n = x.shape
    n_in = len(a_list)
    ka = a_list[0].shape[1]
    return pl.pallas_call(
        functools.partial(_proj_norm_res_kernel, n_in=n_in),
        grid=(m // tm,),
        in_specs=([pl.BlockSpec((tm, ka), lambda i: (i, 0)) for _ in range(n_in)]
                  + [pl.BlockSpec((ka, n), lambda i, s=s: (s, 0)) for s in range(n_in)]
                  + [pl.BlockSpec((1, n), lambda i: (0, 0)), pl.BlockSpec((tm, n), lambda i: (i, 0))]),
        out_specs=pl.BlockSpec((tm, n), lambda i: (i, 0)),
        out_shape=jax.ShapeDtypeStruct((m, n), F32),
        compiler_params=_cparams(("parallel",)),
        name=name,
    )(*a_list, *([w] * n_in), gain, x)


def _cross_kernel(q_ref, mk_ref, mv_ref, o_ref):
    for h in range(N_XHEADS):
        cols = slice(h * HEAD_W, (h + 1) * HEAD_W)
        s = _dot_nt(q_ref[0, :, cols].astype(BF16), mk_ref[0, :, cols].astype(BF16)) * (HEAD_W ** -0.5)
        s = s - jnp.max(s, axis=-1, keepdims=True)
        e = jnp.exp(s)
        p = e / jnp.sum(e, axis=-1, keepdims=True)
        o_ref[0, :, cols] = _dot(p.astype(BF16), mv_ref[0, :, cols].astype(BF16)).astype(o_ref.dtype)


def _cross_attend(q, mk, mv, *, tq, out_dtype, name):
    b, t, w = q.shape
    n_mem = mk.shape[1]
    return pl.pallas_call(
        _cross_kernel,
        grid=(b, t // tq),
        in_specs=[pl.BlockSpec((1, tq, w), lambda i, j: (i, j, 0)),
                  pl.BlockSpec((1, n_mem, w), lambda i, j: (i, 0, 0)),
                  pl.BlockSpec((1, n_mem, w), lambda i, j: (i, 0, 0))],
        out_specs=pl.BlockSpec((1, tq, w), lambda i, j: (i, j, 0)),
        out_shape=jax.ShapeDtypeStruct((b, t, w), out_dtype),
        compiler_params=_cparams(("parallel", "parallel")),
        name=name,
    )(q, mk, mv)


def _cross_interleaved_kernel(q_ref, mk_ref, mv_ref, o_ref):
    tq = q_ref.shape[1]
    q = q_ref[0]
    qs = jnp.concatenate([q[:, h * HEAD_W:(h + 1) * HEAD_W] for h in range(N_XHEADS)], axis=0).astype(BF16)
    s = _dot_nt(qs, mk_ref[0].astype(BF16)) * (HEAD_W ** -0.5)
    row_head = lax.broadcasted_iota(jnp.int32, s.shape, 0) // tq
    col_head = lax.broadcasted_iota(jnp.int32, s.shape, 1) % N_XHEADS
    s = jnp.where(row_head == col_head, s, NEG_BIG)
    e = jnp.exp(s - jnp.max(s, axis=-1, keepdims=True))
    p = e / jnp.sum(e, axis=-1, keepdims=True)
    o = _dot(p.astype(BF16), mv_ref[0].astype(BF16))
    for h in range(N_XHEADS):
        o_ref[0, :, h * HEAD_W:(h + 1) * HEAD_W] = o[h * tq:(h + 1) * tq, :].astype(o_ref.dtype)


def _cross_attend_interleaved(q, mk, mv, *, name):
    b, t, w = q.shape
    rows = mk.shape[1]
    return pl.pallas_call(
        _cross_interleaved_kernel,
        grid=(b,),
        in_specs=[pl.BlockSpec((1, t, w), lambda i: (i, 0, 0)),
                  pl.BlockSpec((1, rows, HEAD_W), lambda i: (i, 0, 0)),
                  pl.BlockSpec((1, rows, HEAD_W), lambda i: (i, 0, 0))],
        out_specs=pl.BlockSpec((1, t, w), lambda i: (i, 0, 0)),
        out_shape=jax.ShapeDtypeStruct((b, t, w), q.dtype),
        compiler_params=_cparams(("parallel",)),
        name=name,
    )(q, mk, mv)


def _ffn_kernel(x_ref, gpre_ref, wg_ref, wu_ref, wd_ref, gpost_ref, o_ref, h_ref, acc_ref):
    f = pl.program_id(1)

    @pl.when(f == 0)
    def _():
        h_ref[...] = _rms(x_ref[...], gpre_ref[...]).astype(BF16)
        acc_ref[...] = jnp.zeros(acc_ref.shape, F32)

    h = h_ref[...]
    g = _dot(h, wg_ref[...])
    a = (g * jax.nn.sigmoid(g)) * _dot(h, wu_ref[...])
    acc_ref[...] += _dot(a.astype(BF16), wd_ref[...])

    @pl.when(f == pl.num_programs(1) - 1)
    def _():
        o_ref[...] = x_ref[...] + _rms(acc_ref[...], gpost_ref[...])


def _ffn(x, gpre, wg, wu, wd, gpost, *, tm, tf, name):
    m, d = x.shape
    dff = wg.shape[1]
    return pl.pallas_call(
        _ffn_kernel,
        grid=(m // tm, dff // tf),
        in_specs=[pl.BlockSpec((tm, d), lambda i, f: (i, 0)),
                  pl.BlockSpec((1, d), lambda i, f: (0, 0)),
                  pl.BlockSpec((d, tf), lambda i, f: (0, f)),
                  pl.BlockSpec((d, tf), lambda i, f: (0, f)),
                  pl.BlockSpec((tf, d), lambda i, f: (f, 0)),
                  pl.BlockSpec((1, d), lambda i, f: (0, 0))],
        out_specs=pl.BlockSpec((tm, d), lambda i, f: (i, 0)),
        out_shape=jax.ShapeDtypeStruct((m, d), F32),
        scratch_shapes=[pltpu.VMEM((tm, d), BF16), pltpu.VMEM((tm, d), F32)],
        compiler_params=_cparams(("parallel", "arbitrary")),
        name=name,
    )(x, gpre, wg, wu, wd, gpost)


def _rope_tables(pos):
    inv_freq = ROPE_BASE ** (-jnp.arange(0, HEAD_W, 2, dtype=F32) / HEAD_W)
    ang = pos.astype(F32)[:, None] * inv_freq[None, :]
    cos = jnp.repeat(jnp.cos(ang), 2, axis=-1)
    sin = jnp.sin(ang)
    return cos, jnp.stack([-sin, sin], axis=-1).reshape(cos.shape)


def _layer(x, pos, s0, mem_k, mem_v, paged, w, *, batch, seq, lam_init, tag):
    m = batch * seq
    big = paged is None
    tm = 512 if big else m
    proj_a, dk, dv = _in_proj(x, w["n_pre_mix"], w["w_in"], tm=1024 if big else m, tn=512, name=f"in_proj_{tag}")
    cos, sin = _rope_tables(pos)
    ret_o, ret_s = _retention(proj_a, cos, sin, w["lg"], w["ret_gn"], s0, batch=batch, seq=seq,
                              heads=1 if big else N_HEADS, out_dtype=BF16 if big else F32, name=f"retention_{tag}")
    if big:
        diff_o = _diff_prompt(proj_a, dk, dv, w["rel_bias"], w["lamp"], w["diff_gn"], batch=batch, seq=seq,
                              t=512, lam_init=lam_init, name=f"diff_attn_{tag}")
    else:
        diff_o = _diff_sample(proj_a, dk, dv, *paged, w["rel_bias"], w["lamp"], w["diff_gn"], t_new=seq,
                              lam_init=lam_init, name=f"diff_attn_{tag}")
    x = _proj_norm_res([ret_o, diff_o], w["w_out"], w["n_post_mix"], x, tm=tm, name=f"out_proj_{tag}")
    xw = N_XHEADS * HEAD_W
    q = _norm_matmul(x, w["n_pre_x"], w["w_xq"], tm=tm, tn=xw, out_dtype=BF16 if big else F32, name=f"xq_{tag}")
    if big:
        o = _cross_attend(q.reshape(batch, seq, xw), mem_k, mem_v, tq=512, out_dtype=BF16, name=f"cross_{tag}")
    else:
        o = _cross_attend_interleaved(q.reshape(batch, seq, xw), mem_k, mem_v, name=f"cross_{tag}")
    x = _proj_norm_res([o.reshape(m, xw)], w["w_xo"], w["n_post_x"], x, tm=tm, name=f"xo_{tag}")
    x = _ffn(x, w["n_pre_ffn"], w["w_gate"], w["w_up"], w["w_down"], w["n_post_ffn"], tm=tm, tf=512,
             name=f"ffn_{tag}")
    return x, dk, dv, ret_s


def kernel(x_prompt, x_sample, cache_k, cache_v, state_ret, cache_mem_k, cache_mem_v, page_table, mem_prompt,
           rel_bias, norm_pre_mix, norm_post_mix, norm_pre_x, norm_post_x, norm_pre_ffn, norm_post_ffn, norm_mem,
           w_in, w_out, ret_gn, diff_gn, lam_q1, lam_k1, lam_q2, lam_k2,
           w_xq, w_xk, w_xv, w_xo, w_gate, w_up, w_down):
    b_p, t_p, d = x_prompt.shape
    b_s, t_s, _ = x_sample.shape
    depth = w_in.shape[0]
    n_mem = mem_prompt.shape[1]
    xw = N_XHEADS * HEAD_W
    past_len = page_table.shape[1] * cache_k.shape[2]
    pos_p = jnp.arange(t_p)
    pos_s = past_len + jnp.arange(t_s)
    log_g = jnp.log1p(-(2.0 ** (-5.0 - jnp.arange(N_HEADS, dtype=F32))))
    lg = jnp.broadcast_to(log_g[:, None, None], (N_HEADS, 8, HEAD_W))
    xp = x_prompt.reshape(b_p * t_p, d)
    xs = x_sample.reshape(b_s * t_s, d)
    outs = [[] for _ in range(8)]
    for l in range(depth):
        lam_init = 0.8 - 0.6 * math.exp(-0.3 * l)
        w = {
            "n_pre_mix": norm_pre_mix[l][None], "n_post_mix": norm_post_mix[l][None],
            "n_pre_x": norm_pre_x[l][None], "n_post_x": norm_post_x[l][None],
            "n_pre_ffn": norm_pre_ffn[l][None], "n_post_ffn": norm_post_ffn[l][None],
            "w_in": w_in[l].astype(BF16), "w_out": w_out[l].astype(BF16),
            "ret_gn": ret_gn[l][None], "diff_gn": diff_gn[l][None],
            "lamp": jnp.stack([lam_q1[l], lam_k1[l], lam_q2[l], lam_k2[l]]),
            "w_xq": w_xq[l].astype(BF16), "w_xo": w_xo[l].astype(BF16),
            "w_gate": w_gate[l].astype(BF16), "w_up": w_up[l].astype(BF16), "w_down": w_down[l].astype(BF16),
            "rel_bias": rel_bias, "lg": lg,
        }
        mem_flat = mem_prompt.reshape(b_p * n_mem, d)
        mk_p = _norm_matmul(mem_flat, norm_mem[l][None], w_xk[l].astype(BF16), tm=b_p * n_mem, tn=xw,
                            out_dtype=F32, name="mem_k")
        mv_p = _norm_matmul(mem_flat, norm_mem[l][None], w_xv[l].astype(BF16), tm=b_p * n_mem, tn=xw,
                            out_dtype=F32, name="mem_v")
        s0 = jnp.zeros((b_p, N_HEADS, HEAD_W, HEAD_W), F32)
        xp, dk_p, dv_p, s_p = _layer(xp, pos_p, s0, mk_p.reshape(b_p, n_mem, xw), mv_p.reshape(b_p, n_mem, xw),
                                     None, w, batch=b_p, seq=t_p, lam_init=lam_init, tag="prompt")
        xs, dk_s, dv_s, s_s = _layer(xs, pos_s, state_ret[l], cache_mem_k[l].reshape(b_s, n_mem * N_XHEADS, HEAD_W),
                                     cache_mem_v[l].reshape(b_s, n_mem * N_XHEADS, HEAD_W),
                                     (cache_k[l], cache_v[l], page_table), w,
                                     batch=b_s, seq=t_s, lam_init=lam_init, tag="sample")
        per_layer = (dk_p.reshape(b_p, t_p, N_HEADS, HEAD_W), dv_p.reshape(b_p, t_p, N_HEADS, HEAD_W), s_p,
                     mk_p.reshape(b_p, n_mem, N_XHEADS, HEAD_W), mv_p.reshape(b_p, n_mem, N_XHEADS, HEAD_W),
                     dk_s.reshape(b_s, t_s, N_HEADS, HEAD_W), dv_s.reshape(b_s, t_s, N_HEADS, HEAD_W), s_s)
        for acc, val in zip(outs, per_layer):
            acc.append(val)
    stacked = [jnp.stack(o) for o in outs]
    return (xp.reshape(b_p, t_p, d), xs.reshape(b_s, t_s, d), *stacked)
```

```python
import functools
import math

import jax
import jax.numpy as jnp
from jax import lax
from jax.experimental import pallas as pl
from jax.experimental.pallas import tpu as pltpu

F32 = jnp.float32
BF16 = jnp.bfloat16

D_MODEL = 2048
N_HEADS = 8
HEAD_W = 128
GROUP_W = N_HEADS * HEAD_W
DIFF_D = 64
ROPE_BASE = 10000.0
RET_CHUNK = 128
N_BUCKETS = 32
MAX_EXACT = N_BUCKETS // 2
MAX_DISTANCE = 128
BIAS_BAND = 128
N_XHEADS = 4
RMS_EPS = 1e-6
NEG_BIG = -1e30
LOG2E = math.log2(math.e)
PROMPT_RB = 32
SAMPLE_RB = 16
PAGES_PER_STEP = 8
RET_UNROLL = 8
N_SECTIONS_A = 5
VMEM_LIMIT = 56 * 1024 * 1024


def _cparams(sem):
    return pltpu.CompilerParams(dimension_semantics=sem, vmem_limit_bytes=VMEM_LIMIT)


def _rms(x, gain):
    return x * lax.rsqrt(jnp.mean(x * x, axis=-1, keepdims=True) + RMS_EPS) * gain


def _dot(a, b):
    return jnp.dot(a, b, preferred_element_type=F32)


def _dot_nt(a, b):
    return lax.dot_general(a, b, (((1,), (1,)), ((), ())), preferred_element_type=F32)


def _dot_tn(a, b):
    return lax.dot_general(a, b, (((0,), (0,)), ((), ())), preferred_element_type=F32)


def _norm_matmul_kernel(x_ref, g_ref, w_ref, o_ref, h_ref):
    @pl.when(pl.program_id(1) == 0)
    def _():
        h_ref[...] = _rms(x_ref[...], g_ref[...]).astype(BF16)

    o_ref[...] = _dot(h_ref[...], w_ref[...]).astype(o_ref.dtype)


def _norm_matmul(x, gain, w, *, tm, tn, out_dtype, name):
    m, k = x.shape
    n = w.shape[1]
    return pl.pallas_call(
        _norm_matmul_kernel,
        grid=(m // tm, n // tn),
        in_specs=[pl.BlockSpec((tm, k), lambda i, j: (i, 0)),
                  pl.BlockSpec((1, k), lambda i, j: (0, 0)),
                  pl.BlockSpec((k, tn), lambda i, j: (0, j))],
        out_specs=pl.BlockSpec((tm, tn), lambda i, j: (i, j)),
        out_shape=jax.ShapeDtypeStruct((m, n), out_dtype),
        scratch_shapes=[pltpu.VMEM((tm, k), BF16)],
        compiler_params=_cparams(("parallel", "arbitrary")),
        name=name,
    )(x, gain, w)


def _in_proj_kernel(x_ref, g_ref, w_ref, a_ref, dk_ref, dv_ref, h_ref, *, n_a, n_g):
    j = pl.program_id(1)

    @pl.when(j == 0)
    def _():
        h_ref[...] = _rms(x_ref[...], g_ref[...]).astype(BF16)

    res = _dot(h_ref[...], w_ref[...])

    @pl.when(j < n_a)
    def _():
        a_ref[...] = res

    @pl.when(jnp.logical_and(j >= n_a, j < n_a + n_g))
    def _():
        dk_ref[...] = res

    @pl.when(j >= n_a + n_g)
    def _():
        dv_ref[...] = res


def _in_proj(x, gain, w, *, tm, tn, name):
    m, k = x.shape
    n = w.shape[1]
    n_g = GROUP_W // tn
    n_a = N_SECTIONS_A * n_g
    return pl.pallas_call(
        functools.partial(_in_proj_kernel, n_a=n_a, n_g=n_g),
        grid=(m // tm, n // tn),
        in_specs=[pl.BlockSpec((tm, k), lambda i, j: (i, 0)),
                  pl.BlockSpec((1, k), lambda i, j: (0, 0)),
                  pl.BlockSpec((k, tn), lambda i, j: (0, j))],
        out_specs=[pl.BlockSpec((tm, tn), lambda i, j: (i, jnp.minimum(j, n_a - 1))),
                   pl.BlockSpec((tm, tn), lambda i, j: (i, jnp.clip(j - n_a, 0, n_g - 1))),
                   pl.BlockSpec((tm, tn), lambda i, j: (i, jnp.clip(j - n_a - n_g, 0, n_g - 1)))],
        out_shape=[jax.ShapeDtypeStruct((m, N_SECTIONS_A * GROUP_W), F32),
                   jax.ShapeDtypeStruct((m, GROUP_W), F32),
                   jax.ShapeDtypeStruct((m, GROUP_W), F32)],
        scratch_shapes=[pltpu.VMEM((tm, k), BF16)],
        compiler_params=_cparams(("parallel", "arbitrary")),
        name=name,
    )(x, gain, w)


def _rotate(x, cos, sin_signed):
    lane = lax.broadcasted_iota(jnp.int32, x.shape, 1)
    nxt = pltpu.roll(x, HEAD_W - 1, 1)
    prv = pltpu.roll(x, 1, 1)
    return x * cos + jnp.where(lane % 2 == 0, nxt, prv) * sin_signed


def _retention_kernel(q_ref, k_ref, v_ref, g_ref, cos_ref, sin_ref, lg_ref, gn_ref, s0_ref,
                      o_ref, sf_ref, *, chunk, n_chunks, heads):
    rows_mm = RET_CHUNK
    ri = lax.broadcasted_iota(jnp.int32, (rows_mm, rows_mm), 0)
    ci = lax.broadcasted_iota(jnp.int32, (rows_mm, rows_mm), 1)
    rel = (ri - ci).astype(F32)
    idx = lax.broadcasted_iota(jnp.int32, (rows_mm, 1), 0).astype(F32)

    def pad(a):
        if chunk == rows_mm:
            return a
        return jnp.concatenate([a, jnp.zeros((rows_mm - chunk, a.shape[1]), a.dtype)], axis=0)

    for hh in range(heads):
        cols = slice(hh * HEAD_W, (hh + 1) * HEAD_W)
        lg = lg_ref[hh, 0:1, 0:1]
        intra = jnp.where(rel >= 0, jnp.exp(lg * jnp.maximum(rel, 0.0)), 0.0)
        q_dec = jnp.exp(lg * (idx + 1.0))
        k_dec = jnp.exp(lg * jnp.maximum(chunk - 1.0 - idx, 0.0))
        chunk_dec = jnp.exp(lg * float(chunk))
        gn = gn_ref[:, cols]

        def body(c, s, cols=cols, intra=intra, q_dec=q_dec, k_dec=k_dec, chunk_dec=chunk_dec, gn=gn):
            if n_chunks == 1:
                rows = slice(0, chunk)
            else:
                rows = pl.ds(pl.multiple_of(c * chunk, chunk), chunk)
            cos = cos_ref[rows, :]
            sin = sin_ref[rows, :]
            q = pad(_rotate(q_ref[rows, cols], cos, sin))
            k = pad(_rotate(k_ref[rows, cols], cos, sin) * (HEAD_W ** -0.5))
            qb = q.astype(BF16)
            vb = pad(v_ref[rows, cols]).astype(BF16)
            scores = _dot_nt(qb, k.astype(BF16)) * intra
            o = _dot(scores.astype(BF16), vb) + _dot(qb, s.astype(BF16)) * q_dec
            s_new = s * chunk_dec + _dot_tn((k * k_dec).astype(BF16), vb)
            gate = g_ref[rows, cols]
            y = _rms(o[0:chunk, :], gn)
            o_ref[rows, cols] = ((gate * jax.nn.sigmoid(gate)) * y).astype(o_ref.dtype)
            return s_new

        s0 = s0_ref[0, hh]
        if n_chunks == 1:
            s_fin = body(0, s0)
        else:
            s_fin = lax.fori_loop(0, n_chunks, body, s0, unroll=math.gcd(n_chunks, RET_UNROLL))
        sf_ref[0, hh] = s_fin


def _retention(proj_a, cos, sin, lg, gn, s0, *, batch, seq, heads, out_dtype, name):
    chunk = math.gcd(seq, RET_CHUNK)
    w = heads * HEAD_W
    n_hb = N_HEADS // heads

    def sec(s):
        return pl.BlockSpec((seq, w), lambda b, h, s=s: (b, s * n_hb + h))

    return pl.pallas_call(
        functools.partial(_retention_kernel, chunk=chunk, n_chunks=seq // chunk, heads=heads),
        grid=(batch, n_hb),
        in_specs=[sec(0), sec(1), sec(2), sec(3),
                  pl.BlockSpec((seq, HEAD_W), lambda b, h: (0, 0)),
                  pl.BlockSpec((seq, HEAD_W), lambda b, h: (0, 0)),
                  pl.BlockSpec((heads, 8, HEAD_W), lambda b, h: (h, 0, 0)),
                  pl.BlockSpec((1, w), lambda b, h: (0, h)),
                  pl.BlockSpec((1, heads, HEAD_W, HEAD_W), lambda b, h: (b, h, 0, 0))],
        out_specs=[pl.BlockSpec((seq, w), lambda b, h: (b, h)),
                   pl.BlockSpec((1, heads, HEAD_W, HEAD_W), lambda b, h: (b, h, 0, 0))],
        out_shape=[jax.ShapeDtypeStruct((batch * seq, GROUP_W), out_dtype),
                   jax.ShapeDtypeStruct((batch, N_HEADS, HEAD_W, HEAD_W), F32)],
        compiler_params=_cparams(("parallel", "parallel")),
        name=name,
    )(proj_a, proj_a, proj_a, proj_a, cos, sin, lg, gn, s0)


def _t5_bucket(qpos, kpos):
    n = jnp.maximum(qpos[:, None] - kpos[None, :], 0)
    nf = jnp.maximum(n, 1).astype(F32)
    large = MAX_EXACT + (jnp.log(nf / MAX_EXACT) / math.log(MAX_DISTANCE / MAX_EXACT)
                         * (N_BUCKETS - MAX_EXACT)).astype(jnp.int32)
    large = jnp.minimum(large, N_BUCKETS - 1)
    return jnp.where(n < MAX_EXACT, n, large)


def _bias_from_buckets(bkt, rb_ref, h):
    far = rb_ref[N_BUCKETS - 1, h]
    b = jnp.zeros(bkt.shape, F32)
    for u in range(N_BUCKETS - 1):
        b = jnp.where(bkt == u, rb_ref[u, h] - far, b)
    return b


def _lam(lamp_ref, lam_init):
    e1 = jnp.exp(jnp.sum(lamp_ref[0:1, :] * lamp_ref[1:2, :], axis=-1, keepdims=True))
    e2 = jnp.exp(jnp.sum(lamp_ref[2:3, :] * lamp_ref[3:4, :], axis=-1, keepdims=True))
    return e1 - e2 + lam_init


def _softmax_tile(s_ref, p_ref, m_ref, alpha_ref, add_fn, *, rb, keep_in_regs, cols=None):
    n_rows = s_ref.shape[0]
    col0, n_cols = (0, s_ref.shape[1]) if cols is None else cols
    n_slabs = n_cols // HEAD_W
    for r0 in range(0, n_rows, rb):
        r = slice(r0, r0 + rb)
        m_prev = m_ref[r, :]
        vals, part = [], None
        for j in range(n_slabs):
            c = slice(col0 + j * HEAD_W, col0 + (j + 1) * HEAD_W)
            s = s_ref[r, c]
            if add_fn is not None:
                s = s + add_fn(r0, j)
                if not keep_in_regs:
                    s_ref[r, c] = s
            if keep_in_regs:
                vals.append(s)
            part = s if part is None else jnp.maximum(part, s)
        m_new = jnp.maximum(m_prev, jnp.max(part, axis=-1, keepdims=True))
        for j in range(n_slabs):
            c = slice(col0 + j * HEAD_W, col0 + (j + 1) * HEAD_W)
            s = vals[j] if keep_in_regs else s_ref[r, c]
            p_ref[r, c] = jnp.exp2(s - m_new).astype(BF16)
        alpha_ref[r, :] = jnp.exp2(m_prev - m_new)
        m_ref[r, :] = m_new


def _accumulate(acc_ref, alpha_ref, p, v1):
    a = alpha_ref[...]
    acc_ref[...] = jnp.concatenate([a, a], axis=1) * acc_ref[...] + _dot(p, v1)


def _diff_prompt_kernel(q_ref, k_ref, v_ref, bkt_ref, rb_ref, lamp_ref, gn_ref, o_ref,
                        kbf, v1, qs, s_a, s_b, p_a, p_b, m_ref, alpha_a, alpha_b, acc_ref, bias_ref,
                        *, t, lam_init):
    h = pl.program_id(0)
    b = pl.program_id(1)
    qi = pl.program_id(2)

    @pl.when(qi == 0)
    def _():
        kbf[...] = k_ref[...].astype(BF16)
        v1[:, 0:HEAD_W] = v_ref[...].astype(BF16)
        v1[:, HEAD_W:2 * HEAD_W] = jnp.ones((v1.shape[0], HEAD_W), BF16)

    @pl.when(jnp.logical_and(b == 0, qi == 0))
    def _():
        ri = lax.broadcasted_iota(jnp.int32, (t, t), 0)
        ci = lax.broadcasted_iota(jnp.int32, (t, t), 1)
        bias_ref[0] = jnp.where(ci <= ri, 0.0, NEG_BIG)
        bias_ref[1] = jnp.zeros((t, t), F32)
        bias_ref[2] = jnp.zeros((t, t), F32)
        bri = lax.broadcasted_iota(jnp.int32, (BIAS_BAND, BIAS_BAND), 0)
        bci = lax.broadcasted_iota(jnp.int32, (BIAS_BAND, BIAS_BAND), 1)
        for rk in range(t // BIAS_BAND):
            r = slice(rk * BIAS_BAND, (rk + 1) * BIAS_BAND)
            bias_ref[0, r, r] = jnp.where(bci <= bri, _bias_from_buckets(bkt_ref[0, r, r], rb_ref, h) * LOG2E, NEG_BIG)
            if rk > 0:
                c = slice((rk - 1) * BIAS_BAND, rk * BIAS_BAND)
                bias_ref[0, r, c] = _bias_from_buckets(bkt_ref[0, r, c], rb_ref, h) * LOG2E
        r, c = slice(0, BIAS_BAND), slice(t - BIAS_BAND, t)
        bias_ref[1, r, c] = _bias_from_buckets(bkt_ref[1, r, c], rb_ref, h) * LOG2E

    q = q_ref[...] * (DIFF_D ** -0.5 * LOG2E)
    lane = lax.broadcasted_iota(jnp.int32, q.shape, 1)
    qs[0:t, :] = jnp.where(lane < DIFF_D, q, 0.0).astype(BF16)
    qs[t:2 * t, :] = jnp.where(lane >= DIFF_D, q, 0.0).astype(BF16)
    m_ref[...] = jnp.full(m_ref.shape, -jnp.inf, F32)
    acc_ref[...] = jnp.zeros(acc_ref.shape, F32)

    def scores(k, s_buf):
        rows = pl.ds(pl.multiple_of(jnp.minimum(k, qi) * t, t), t)
        s_buf[...] = _dot_nt(qs[...], kbf[rows, :])

    def update(k, s_buf, p_buf, alpha_buf):
        which = jnp.where(k == qi, 0, jnp.where(k == qi - 1, 1, 2))

        def add_fn(r0, j):
            q0 = r0 % t
            return bias_ref[which, q0:q0 + PROMPT_RB, j * HEAD_W:(j + 1) * HEAD_W]

        _softmax_tile(s_buf, p_buf, m_ref, alpha_buf, add_fn, rb=PROMPT_RB, keep_in_regs=True)
        rows = pl.ds(pl.multiple_of(k * t, t), t)
        _accumulate(acc_ref, alpha_buf, p_buf[...], v1[rows, :])

    n_tiles = qi + 1
    scores(0, s_a)

    def pair(j, carry):
        k0 = 2 * j
        scores(k0 + 1, s_b)
        update(k0, s_a, p_a, alpha_a)
        scores(k0 + 2, s_a)
        update(k0 + 1, s_b, p_b, alpha_b)
        return carry

    lax.fori_loop(0, n_tiles // 2, pair, 0)

    @pl.when(n_tiles % 2 == 1)
    def _():
        update(qi, s_a, p_a, alpha_a)

    lam = _lam(lamp_ref, lam_init)
    o = (acc_ref[0:t, 0:HEAD_W] / acc_ref[0:t, HEAD_W:2 * HEAD_W]
         - lam * (acc_ref[t:2 * t, 0:HEAD_W] / acc_ref[t:2 * t, HEAD_W:2 * HEAD_W]))
    o_ref[...] = (_rms(o, gn_ref[...]) * (1.0 - lam_init)).astype(o_ref.dtype)


def _diff_prompt(proj_a, dk, dv, rel_bias, lamp, gn, *, batch, seq, t, lam_init, name):
    nq = seq // t
    assert t % BIAS_BAND == 0 and BIAS_BAND >= MAX_DISTANCE
    pos = jnp.arange(t)
    bkt = jnp.stack([_t5_bucket(pos, pos), _t5_bucket(pos + t, pos)]).astype(jnp.int32)
    q_sec = (N_SECTIONS_A - 1) * N_HEADS
    return pl.pallas_call(
        functools.partial(_diff_prompt_kernel, t=t, lam_init=lam_init),
        grid=(N_HEADS, batch, nq),
        in_specs=[pl.BlockSpec((t, HEAD_W), lambda h, b, i: (b * nq + i, q_sec + h)),
                  pl.BlockSpec((seq, HEAD_W), lambda h, b, i: (b, h)),
                  pl.BlockSpec((seq, HEAD_W), lambda h, b, i: (b, h)),
                  pl.BlockSpec((2, t, t), lambda h, b, i: (0, 0, 0)),
                  pl.BlockSpec(memory_space=pltpu.SMEM),
                  pl.BlockSpec((4, DIFF_D), lambda h, b, i: (0, 0)),
                  pl.BlockSpec((1, HEAD_W), lambda h, b, i: (0, h))],
        out_specs=pl.BlockSpec((t, HEAD_W), lambda h, b, i: (b * nq + i, h)),
        out_shape=jax.ShapeDtypeStruct((batch * seq, GROUP_W), BF16),
        scratch_shapes=[pltpu.VMEM((seq, HEAD_W), BF16), pltpu.VMEM((seq, 2 * HEAD_W), BF16),
                        pltpu.VMEM((2 * t, HEAD_W), BF16),
                        pltpu.VMEM((2 * t, t), F32), pltpu.VMEM((2 * t, t), F32),
                        pltpu.VMEM((2 * t, t), BF16), pltpu.VMEM((2 * t, t), BF16),
                        pltpu.VMEM((2 * t, HEAD_W), F32), pltpu.VMEM((2 * t, HEAD_W), F32),
                        pltpu.VMEM((2 * t, HEAD_W), F32), pltpu.VMEM((2 * t, 2 * HEAD_W), F32),
                        pltpu.VMEM((3, t, t), F32)],
        compiler_params=_cparams(("arbitrary", "arbitrary", "arbitrary")),
        name=name,
    )(proj_a, dk, dv, bkt, rel_bias, lamp, gn)


def _diff_sample_kernel(pt_ref, q_ref, *refs, t_new, lam_init, n_pg):
    del pt_ref
    k_refs, v_refs = refs[:n_pg], refs[n_pg:2 * n_pg]
    (kn_ref, vn_ref, bkt_last_ref, bkt_new_ref, rb_ref, lamp_ref, gn_ref, o_ref,
     qf, qbf, v1all, knew, v1new, s_ref, p_ref, sn_ref, pn_ref, m_ref, alpha_ref, acc_ref,
     mask_ref, bias_last_ref, bias_new_ref) = refs[2 * n_pg:]
    b = pl.program_id(0)
    p = pl.program_id(1)
    last = pl.num_programs(1) - 1
    rows_per_head = 2 * t_new
    n_new = t_new * N_HEADS
    pg_rows = mask_ref.shape[1]

    def per_head_bias(bkt_ref):
        parts = []
        for h in range(N_HEADS):
            r0 = h * rows_per_head
            bk = bkt_ref[r0:r0 + rows_per_head, :]
            parts.append(jnp.where(bk >= 0, _bias_from_buckets(bk, rb_ref, h) * LOG2E, NEG_BIG))
        return jnp.concatenate(parts, axis=0)

    @pl.when(jnp.logical_and(b == 0, p == 0))
    def _():
        mask_ref[...] = jnp.where(bkt_last_ref[...] >= 0, 0.0, NEG_BIG)
        bias_last_ref[...] = per_head_bias(bkt_last_ref)
        bias_new_ref[...] = per_head_bias(bkt_new_ref)
        knew[...] = jnp.zeros(knew.shape, BF16)
        v1new[:, 0:HEAD_W] = jnp.zeros((v1new.shape[0], HEAD_W), BF16)
        v1new[:, HEAD_W:2 * HEAD_W] = jnp.ones((v1new.shape[0], HEAD_W), BF16)
        v1all[:, HEAD_W:2 * HEAD_W] = jnp.ones((v1all.shape[0], HEAD_W), BF16)

    @pl.when(p == 0)
    def _():
        q = q_ref[...] * (DIFF_D ** -0.5 * LOG2E)
        lane = lax.broadcasted_iota(jnp.int32, (t_new, HEAD_W), 1)
        for h in range(N_HEADS):
            cols = slice(h * HEAD_W, (h + 1) * HEAD_W)
            r0 = h * rows_per_head
            qf[r0:r0 + t_new, :] = jnp.where(lane < DIFF_D, q[:, cols], 0.0)
            qf[r0 + t_new:r0 + 2 * t_new, :] = jnp.where(lane >= DIFF_D, q[:, cols], 0.0)
        qbf[...] = qf[...].astype(BF16)
        m_ref[...] = jnp.full(m_ref.shape, -jnp.inf, F32)
        acc_ref[...] = jnp.zeros(acc_ref.shape, F32)

    def add_mask(r0, j):
        return mask_ref[r0:r0 + SAMPLE_RB, j * HEAD_W:(j + 1) * HEAD_W]

    def add_last(r0, j):
        return bias_last_ref[r0:r0 + SAMPLE_RB, j * HEAD_W:(j + 1) * HEAD_W]

    def page_update(g, add_fn):
        pg = slice(g * pg_rows, (g + 1) * pg_rows)
        _softmax_tile(s_ref, p_ref, m_ref, alpha_ref.at[g], add_fn, rb=SAMPLE_RB, keep_in_regs=False,
                      cols=(g * pg_rows, pg_rows))
        _accumulate(acc_ref, alpha_ref.at[g], p_ref[:, pg], v1all[pg, :])

    def page_scores(g):
        pg = slice(g * pg_rows, (g + 1) * pg_rows)
        v1all[pg, 0:HEAD_W] = v_refs[g][0].astype(BF16)
        s_ref[:, pg] = _dot_nt(qbf[...], k_refs[g][0].astype(BF16))

    page_scores(0)
    for g in range(n_pg - 1):
        page_scores(g + 1)
        page_update(g, add_mask)

    @pl.when(p < last)
    def _():
        page_update(n_pg - 1, add_mask)

    @pl.when(p == last)
    def _():
        page_update(n_pg - 1, add_last)
        knew[0:n_new, :] = kn_ref[0].astype(BF16)
        v1new[0:n_new, 0:HEAD_W] = vn_ref[0].astype(BF16)
        sn_ref[...] = _dot_nt(qbf[...], knew[...])
        _softmax_tile(sn_ref, pn_ref, m_ref, alpha_ref.at[0],
                      lambda r0, j: bias_new_ref[r0:r0 + SAMPLE_RB, :], rb=SAMPLE_RB, keep_in_regs=True)
        _accumulate(acc_ref, alpha_ref.at[0], pn_ref[...], v1new[...])

        lam = _lam(lamp_ref, lam_init)
        for h in range(N_HEADS):
            cols = slice(h * HEAD_W, (h + 1) * HEAD_W)
            r0 = h * rows_per_head
            r1 = r0 + t_new
            o = (acc_ref[r0:r1, 0:HEAD_W] / acc_ref[r0:r1, HEAD_W:2 * HEAD_W]
                 - lam * (acc_ref[r1:r1 + t_new, 0:HEAD_W] / acc_ref[r1:r1 + t_new, HEAD_W:2 * HEAD_W]))
            o_ref[:, cols] = (_rms(o, gn_ref[:, cols]) * (1.0 - lam_init)).astype(o_ref.dtype)


def _diff_sample(proj_a, dk, dv, cache_k, cache_v, page_table, rel_bias, lamp, gn, *, t_new, lam_init, name):
    batch, n_pages = page_table.shape
    n_phys, page = cache_k.shape[0], cache_k.shape[1]
    past = n_pages * page
    rows = N_HEADS * 2 * t_new
    kv_rows = page * N_HEADS
    new_rows = 128
    assert t_new * N_HEADS <= new_rows
    q_head = jnp.repeat(jnp.arange(N_HEADS), 2 * t_new)
    q_pos = past + jnp.tile(jnp.arange(t_new), 2 * N_HEADS)

    def buckets(n_cols, first_key_pos, n_valid_cols):
        col = jnp.arange(n_cols)
        k_pos = first_key_pos + col // N_HEADS
        ok = ((col % N_HEADS)[None, :] == q_head[:, None]) & (k_pos[None, :] <= q_pos[:, None])
        ok = ok & (col < n_valid_cols)[None, :]
        return jnp.where(ok, _t5_bucket(q_pos, k_pos), -1).astype(jnp.int32)

    bkt_last = buckets(kv_rows, past - page, kv_rows)
    bkt_new = buckets(new_rows, past, t_new * N_HEADS)
    n_pg = math.gcd(n_pages, PAGES_PER_STEP)

    def page_spec(g):
        return pl.BlockSpec((1, kv_rows, HEAD_W), lambda b, p, pt, g=g: (pt[b, p * n_pg + g], 0, 0))

    grid_spec = pltpu.PrefetchScalarGridSpec(
        num_scalar_prefetch=1,
        grid=(batch, n_pages // n_pg),
        in_specs=([pl.BlockSpec((t_new, GROUP_W), lambda b, p, pt: (b, N_SECTIONS_A - 1))]
                  + [page_spec(g) for g in range(n_pg)] + [page_spec(g) for g in range(n_pg)]
                  + [pl.BlockSpec((1, t_new * N_HEADS, HEAD_W), lambda b, p, pt: (b, 0, 0)),
                     pl.BlockSpec((1, t_new * N_HEADS, HEAD_W), lambda b, p, pt: (b, 0, 0)),
                     pl.BlockSpec((rows, kv_rows), lambda b, p, pt: (0, 0)),
                     pl.BlockSpec((rows, new_rows), lambda b, p, pt: (0, 0)),
                     pl.BlockSpec(memory_space=pltpu.SMEM),
                     pl.BlockSpec((4, DIFF_D), lambda b, p, pt: (0, 0)),
                     pl.BlockSpec((1, GROUP_W), lambda b, p, pt: (0, 0))]),
        out_specs=pl.BlockSpec((t_new, GROUP_W), lambda b, p, pt: (b, 0)),
        scratch_shapes=[pltpu.VMEM((rows, HEAD_W), F32), pltpu.VMEM((rows, HEAD_W), BF16),
                        pltpu.VMEM((n_pg * kv_rows, 2 * HEAD_W), BF16),
                        pltpu.VMEM((new_rows, HEAD_W), BF16), pltpu.VMEM((new_rows, 2 * HEAD_W), BF16),
                        pltpu.VMEM((rows, n_pg * kv_rows), F32), pltpu.VMEM((rows, n_pg * kv_rows), BF16),
                        pltpu.VMEM((rows, new_rows), F32), pltpu.VMEM((rows, new_rows), BF16),
                        pltpu.VMEM((rows, HEAD_W), F32), pltpu.VMEM((n_pg, rows, HEAD_W), F32),
                        pltpu.VMEM((rows, 2 * HEAD_W), F32),
                        pltpu.VMEM((rows, kv_rows), F32), pltpu.VMEM((rows, kv_rows), F32),
                        pltpu.VMEM((rows, new_rows), F32)],
    )
    ck = cache_k.reshape(n_phys, kv_rows, HEAD_W)
    cv = cache_v.reshape(n_phys, kv_rows, HEAD_W)
    return pl.pallas_call(
        functools.partial(_diff_sample_kernel, t_new=t_new, lam_init=lam_init, n_pg=n_pg),
        grid_spec=grid_spec,
        out_shape=jax.ShapeDtypeStruct((batch * t_new, GROUP_W), F32),
        compiler_params=_cparams(("arbitrary", "arbitrary")),
        name=name,
    )(page_table, proj_a, *([ck] * n_pg), *([cv] * n_pg),
      dk.reshape(batch, t_new * N_HEADS, HEAD_W), dv.reshape(batch, t_new * N_HEADS, HEAD_W),
      bkt_last, bkt_new, rel_bias, lamp, gn)


def _mix_out_kernel(ret_ref, diff_ref, w_ret_ref, w_diff_ref, gain_ref, x_ref, gain_q_ref, wq_ref, *refs, has_mem):
    if has_mem:
        mk_ref, mv_ref, x1_ref, o_ref = refs
    else:
        x1_ref, o_ref = refs
    y = _dot(ret_ref[...].astype(BF16), w_ret_ref[...]) + _dot(diff_ref[...].astype(BF16), w_diff_ref[...])
    x1 = x_ref[...] + _rms(y, gain_ref[...])
    x1_ref[...] = x1
    q = _dot(_rms(x1, gain_q_ref[...]).astype(BF16), wq_ref[...])
    if not has_mem:
        o_ref[...] = q.astype(o_ref.dtype)
        return
    for h in range(N_XHEADS):
        cols = slice(h * HEAD_W, (h + 1) * HEAD_W)
        s = _dot_nt(q[:, cols].astype(BF16), mk_ref[0, :, cols].astype(BF16)) * (HEAD_W ** -0.5)
        e = jnp.exp(s - jnp.max(s, axis=-1, keepdims=True))
        p = e / jnp.sum(e, axis=-1, keepdims=True)
        o_ref[:, cols] = _dot(p.astype(BF16), mv_ref[0, :, cols].astype(BF16)).astype(o_ref.dtype)


def _mix_out(ret_o, diff_o, w_out, gain, x, gain_q, w_xq, mem, *, tm, rows_per_batch, out_dtype, name):
    m, n = x.shape
    ka = ret_o.shape[1]
    xw = w_xq.shape[1]
    in_specs = [pl.BlockSpec((tm, ka), lambda i: (i, 0)), pl.BlockSpec((tm, ka), lambda i: (i, 0)),
                pl.BlockSpec((ka, n), lambda i: (0, 0)), pl.BlockSpec((ka, n), lambda i: (1, 0)),
                pl.BlockSpec((1, n), lambda i: (0, 0)), pl.BlockSpec((tm, n), lambda i: (i, 0)),
                pl.BlockSpec((1, n), lambda i: (0, 0)), pl.BlockSpec((n, xw), lambda i: (0, 0))]
    args = [ret_o, diff_o, w_out, w_out, gain, x, gain_q, w_xq]
    if mem is not None:
        assert rows_per_batch % tm == 0
        per = rows_per_batch // tm
        n_mem = mem[0].shape[1]
        in_specs += [pl.BlockSpec((1, n_mem, xw), lambda i: (i // per, 0, 0))] * 2
        args += list(mem)
    return pl.pallas_call(
        functools.partial(_mix_out_kernel, has_mem=mem is not None),
        grid=(m // tm,),
        in_specs=in_specs,
        out_specs=[pl.BlockSpec((tm, n), lambda i: (i, 0)), pl.BlockSpec((tm, xw), lambda i: (i, 0))],
        out_shape=[jax.ShapeDtypeStruct((m, n), F32), jax.ShapeDtypeStruct((m, xw), out_dtype)],
        compiler_params=_cparams(("parallel",)),
        name=name,
    )(*args)


def _cross_interleaved_kernel(q_ref, mk_ref, mv_ref, o_ref):
    tq = q_ref.shape[1]
    q = q_ref[0]
    qs = jnp.concatenate([q[:, h * HEAD_W:(h + 1) * HEAD_W] for h in range(N_XHEADS)], axis=0).astype(BF16)
    s = _dot_nt(qs, mk_ref[0].astype(BF16)) * (HEAD_W ** -0.5)
    row_head = lax.broadcasted_iota(jnp.int32, s.shape, 0) // tq
    col_head = lax.broadcasted_iota(jnp.int32, s.shape, 1) % N_XHEADS
    s = jnp.where(row_head == col_head, s, NEG_BIG)
    e = jnp.exp(s - jnp.max(s, axis=-1, keepdims=True))
    p = e / jnp.sum(e, axis=-1, keepdims=True)
    o = _dot(p.astype(BF16), mv_ref[0].astype(BF16))
    for h in range(N_XHEADS):
        o_ref[0, :, h * HEAD_W:(h + 1) * HEAD_W] = o[h * tq:(h + 1) * tq, :].astype(o_ref.dtype)


def _cross_attend_interleaved(q, mk, mv, *, name):
    b, t, w = q.shape
    rows = mk.shape[1]
    return pl.pallas_call(
        _cross_interleaved_kernel,
        grid=(b,),
        in_specs=[pl.BlockSpec((1, t, w), lambda i: (i, 0, 0)),
                  pl.BlockSpec((1, rows, HEAD_W), lambda i: (i, 0, 0)),
                  pl.BlockSpec((1, rows, HEAD_W), lambda i: (i, 0, 0))],
        out_specs=pl.BlockSpec((1, t, w), lambda i: (i, 0, 0)),
        out_shape=jax.ShapeDtypeStruct((b, t, w), q.dtype),
        compiler_params=_cparams(("parallel",)),
        name=name,
    )(q, mk, mv)


def _xo_ffn_kernel(x_ref, o_ref, wxo_ref, gx_ref, gpre_ref, wg_ref, wu_ref, wd_ref, gpost_ref, y_ref,
                   x2_ref, h_ref, acc_ref):
    f = pl.program_id(1)

    @pl.when(f == 0)
    def _():
        x2 = x_ref[...] + _rms(_dot(o_ref[...].astype(BF16), wxo_ref[...]), gx_ref[...])
        x2_ref[...] = x2
        h_ref[...] = _rms(x2, gpre_ref[...]).astype(BF16)
        acc_ref[...] = jnp.zeros(acc_ref.shape, F32)

    h = h_ref[...]
    g = _dot(h, wg_ref[...])
    a = (g * jax.nn.sigmoid(g)) * _dot(h, wu_ref[...])
    acc_ref[...] += _dot(a.astype(BF16), wd_ref[...])

    @pl.when(f == pl.num_programs(1) - 1)
    def _():
        y_ref[...] = x2_ref[...] + _rms(acc_ref[...], gpost_ref[...])


def _xo_ffn(x, o, w_xo, gx, gpre, wg, wu, wd, gpost, *, tm, tf, name):
    m, d = x.shape
    xw = o.shape[1]
    dff = wg.shape[1]
    return pl.pallas_call(
        _xo_ffn_kernel,
        grid=(m // tm, dff // tf),
        in_specs=[pl.BlockSpec((tm, d), lambda i, f: (i, 0)),
                  pl.BlockSpec((tm, xw), lambda i, f: (i, 0)),
                  pl.BlockSpec((xw, d), lambda i, f: (0, 0)),
                  pl.BlockSpec((1, d), lambda i, f: (0, 0)),
                  pl.BlockSpec((1, d), lambda i, f: (0, 0)),
                  pl.BlockSpec((d, tf), lambda i, f: (0, f)),
                  pl.BlockSpec((d, tf), lambda i, f: (0, f)),
                  pl.BlockSpec((tf, d), lambda i, f: (f, 0)),
                  pl.BlockSpec((1, d), lambda i, f: (0, 0))],
        out_specs=pl.BlockSpec((tm, d), lambda i, f: (i, 0)),
        out_shape=jax.ShapeDtypeStruct((m, d), F32),
        scratch_shapes=[pltpu.VMEM((tm, d), F32), pltpu.VMEM((tm, d), BF16), pltpu.VMEM((tm, d), F32)],
        compiler_params=_cparams(("parallel", "arbitrary")),
        name=name,
    )(x, o, w_xo, gx, gpre, wg, wu, wd, gpost)


def _rope_tables(pos):
    inv_freq = ROPE_BASE ** (-jnp.arange(0, HEAD_W, 2, dtype=F32) / HEAD_W)
    ang = pos.astype(F32)[:, None] * inv_freq[None, :]
    cos = jnp.repeat(jnp.cos(ang), 2, axis=-1)
    sin = jnp.sin(ang)
    return cos, jnp.stack([-sin, sin], axis=-1).reshape(cos.shape)


def _layer(x, pos, s0, mem_k, mem_v, paged, w, *, batch, seq, lam_init, tag):
    m = batch * seq
    big = paged is None
    tm = 512 if big else m
    proj_a, dk, dv = _in_proj(x, w["n_pre_mix"], w["w_in"], tm=1024 if big else m, tn=512, name=f"in_proj_{tag}")
    cos, sin = _rope_tables(pos)
    ret_o, ret_s = _retention(proj_a, cos, sin, w["lg"], w["ret_gn"], s0, batch=batch, seq=seq,
                              heads=1 if big else N_HEADS, out_dtype=BF16 if big else F32, name=f"retention_{tag}")
    if big:
        diff_o = _diff_prompt(proj_a, dk, dv, w["rel_bias"], w["lamp"], w["diff_gn"], batch=batch, seq=seq,
                              t=512, lam_init=lam_init, name=f"diff_attn_{tag}")
    else:
        diff_o = _diff_sample(proj_a, dk, dv, *paged, w["rel_bias"], w["lamp"], w["diff_gn"], t_new=seq,
                              lam_init=lam_init, name=f"diff_attn_{tag}")
    xw = N_XHEADS * HEAD_W
    x, o = _mix_out(ret_o, diff_o, w["w_out"], w["n_post_mix"], x, w["n_pre_x"], w["w_xq"],
                    (mem_k, mem_v) if big else None, tm=tm, rows_per_batch=seq,
                    out_dtype=BF16 if big else F32, name=f"mix_out_{tag}")
    if not big:
        o = _cross_attend_interleaved(o.reshape(batch, seq, xw), mem_k, mem_v, name=f"cross_{tag}").reshape(m, xw)
    x = _xo_ffn(x, o, w["w_xo"], w["n_post_x"], w["n_pre_ffn"], w["w_gate"], w["w_up"], w["w_down"],
                w["n_post_ffn"], tm=tm, tf=512, name=f"ffn_{tag}")
    return x, dk, dv, ret_s


def kernel(x_prompt, x_sample, cache_k, cache_v, state_ret, cache_mem_k, cache_mem_v, page_table, mem_prompt,
           rel_bias, norm_pre_mix, norm_post_mix, norm_pre_x, norm_post_x, norm_pre_ffn, norm_post_ffn, norm_mem,
           w_in, w_out, ret_gn, diff_gn, lam_q1, lam_k1, lam_q2, lam_k2,
           w_xq, w_xk, w_xv, w_xo, w_gate, w_up, w_down):
    b_p, t_p, d = x_prompt.shape
    b_s, t_s, _ = x_sample.shape
    depth = w_in.shape[0]
    n_mem = mem_prompt.shape[1]
    xw = N_XHEADS * HEAD_W
    past_len = page_table.shape[1] * cache_k.shape[2]
    pos_p = jnp.arange(t_p)
    pos_s = past_len + jnp.arange(t_s)
    log_g = jnp.log1p(-(2.0 ** (-5.0 - jnp.arange(N_HEADS, dtype=F32))))
    lg = jnp.broadcast_to(log_g[:, None, None], (N_HEADS, 8, HEAD_W))
    xp = x_prompt.reshape(b_p * t_p, d)
    xs = x_sample.reshape(b_s * t_s, d)
    outs = [[] for _ in range(8)]
    for l in range(depth):
        lam_init = 0.8 - 0.6 * math.exp(-0.3 * l)
        w = {
            "n_pre_mix": norm_pre_mix[l][None], "n_post_mix": norm_post_mix[l][None],
            "n_pre_x": norm_pre_x[l][None], "n_post_x": norm_post_x[l][None],
            "n_pre_ffn": norm_pre_ffn[l][None], "n_post_ffn": norm_post_ffn[l][None],
            "w_in": w_in[l].astype(BF16), "w_out": w_out[l].astype(BF16),
            "ret_gn": ret_gn[l][None], "diff_gn": diff_gn[l][None],
            "lamp": jnp.stack([lam_q1[l], lam_k1[l], lam_q2[l], lam_k2[l]]),
            "w_xq": w_xq[l].astype(BF16), "w_xo": w_xo[l].astype(BF16),
            "w_gate": w_gate[l].astype(BF16), "w_up": w_up[l].astype(BF16), "w_down": w_down[l].astype(BF16),
            "rel_bias": rel_bias, "lg": lg,
        }
        mem_flat = mem_prompt.reshape(b_p * n_mem, d)
        mk_p = _norm_matmul(mem_flat, norm_mem[l][None], w_xk[l].astype(BF16), tm=b_p * n_mem, tn=xw,
                            out_dtype=F32, name="mem_k")
        mv_p = _norm_matmul(mem_flat, norm_mem[l][None], w_xv[l].astype(BF16), tm=b_p * n_mem, tn=xw,
                            out_dtype=F32, name="mem_v")
        s0 = jnp.zeros((b_p, N_HEADS, HEAD_W, HEAD_W), F32)
        xp, dk_p, dv_p, s_p = _layer(xp, pos_p, s0, mk_p.reshape(b_p, n_mem, xw), mv_p.reshape(b_p, n_mem, xw),
                                     None, w, batch=b_p, seq=t_p, lam_init=lam_init, tag="prompt")
        xs, dk_s, dv_s, s_s = _layer(xs, pos_s, state_ret[l], cache_mem_k[l].reshape(b_s, n_mem * N_XHEADS, HEAD_W),
                                     cache_mem_v[l].reshape(b_s, n_mem * N_XHEADS, HEAD_W),
                                     (cache_k[l], cache_v[l], page_table), w,
                                     batch=b_s, seq=t_s, lam_init=lam_init, tag="sample")
        per_layer = (dk_p.reshape(b_p, t_p, N_HEADS, HEAD_W), dv_p.reshape(b_p, t_p, N_HEADS, HEAD_W), s_p,
                     mk_p.reshape(b_p, n_mem, N_XHEADS, HEAD_W), mv_p.reshape(b_p, n_mem, N_XHEADS, HEAD_W),
                     dk_s.reshape(b_s, t_s, N_HEADS, HEAD_W), dv_s.reshape(b_s, t_s, N_HEADS, HEAD_W), s_s)
        for acc, val in zip(outs, per_layer):
            acc.append(val)
    stacked = [jnp.stack(o) for o in outs]
    return (xp.reshape(b_p, t_p, d), xs.reshape(b_s, t_s, d), *stacked)
```

```python
import functools
import math

import jax
import jax.numpy as jnp
from jax import lax
from jax.experimental import pallas as pl
from jax.experimental.pallas import tpu as pltpu

F32 = jnp.float32
BF16 = jnp.bfloat16

D_MODEL = 2048
N_HEADS = 8
HEAD_W = 128
GROUP_W = N_HEADS * HEAD_W
DIFF_D = 64
ROPE_BASE = 10000.0
RET_CHUNK = 128
N_BUCKETS = 32
MAX_EXACT = N_BUCKETS // 2
MAX_DISTANCE = 128
BIAS_BAND = 128
N_XHEADS = 4
RMS_EPS = 1e-6
NEG_BIG = -1e30
LOG2E = math.log2(math.e)
PROMPT_RB = 32
SAMPLE_RB = 16
PAGES_PER_STEP = 8
RET_UNROLL = 8
N_SECTIONS_A = 5
VMEM_LIMIT = 56 * 1024 * 1024


def _cparams(sem):
    return pltpu.CompilerParams(dimension_semantics=sem, vmem_limit_bytes=VMEM_LIMIT)


def _rms(x, gain):
    return x * lax.rsqrt(jnp.mean(x * x, axis=-1, keepdims=True) + RMS_EPS) * gain


def _dot(a, b):
    return jnp.dot(a, b, preferred_element_type=F32)


def _dot_nt(a, b):
    return lax.dot_general(a, b, (((1,), (1,)), ((), ())), preferred_element_type=F32)


def _dot_tn(a, b):
    return lax.dot_general(a, b, (((0,), (0,)), ((), ())), preferred_element_type=F32)


def _norm_matmul_kernel(x_ref, g_ref, w_ref, o_ref, h_ref):
    @pl.when(pl.program_id(1) == 0)
    def _():
        h_ref[...] = _rms(x_ref[...], g_ref[...]).astype(BF16)

    o_ref[...] = _dot(h_ref[...], w_ref[...].astype(BF16)).astype(o_ref.dtype)


def _norm_matmul(x, gain, w, *, tm, tn, out_dtype, name):
    m, k = x.shape
    n = w.shape[1]
    return pl.pallas_call(
        _norm_matmul_kernel,
        grid=(m // tm, n // tn),
        in_specs=[pl.BlockSpec((tm, k), lambda i, j: (i, 0)),
                  pl.BlockSpec((1, k), lambda i, j: (0, 0)),
                  pl.BlockSpec((k, tn), lambda i, j: (0, j))],
        out_specs=pl.BlockSpec((tm, tn), lambda i, j: (i, j)),
        out_shape=jax.ShapeDtypeStruct((m, n), out_dtype),
        scratch_shapes=[pltpu.VMEM((tm, k), BF16)],
        compiler_params=_cparams(("parallel", "arbitrary")),
        name=name,
    )(x, gain, w)


def _in_proj_kernel(x_ref, g_ref, w_ref, a_ref, dk_ref, dv_ref, *rest, n_a, n_g):
    j = pl.program_id(1)
    h_ref = rest[-1]

    @pl.when(j == 0)
    def _():
        h_ref[...] = _rms(x_ref[...], g_ref[...]).astype(BF16)

    w = w_ref[...]
    if len(rest) == 2:
        w = w.astype(BF16)
        rest[0][...] = w
    res = _dot(h_ref[...], w)

    @pl.when(j < n_a)
    def _():
        a_ref[...] = res

    @pl.when(jnp.logical_and(j >= n_a, j < n_a + n_g))
    def _():
        dk_ref[...] = res

    @pl.when(j >= n_a + n_g)
    def _():
        dv_ref[...] = res


def _in_proj(x, gain, w, *, tm, tn, name):
    m, k = x.shape
    n = w.shape[1]
    n_g = GROUP_W // tn
    n_a = N_SECTIONS_A * n_g
    emit_w = w.dtype != BF16
    assert not emit_w or m == tm, "each weight tile must be visited once to be copied out"
    out_specs = [pl.BlockSpec((tm, tn), lambda i, j: (i, jnp.minimum(j, n_a - 1))),
                 pl.BlockSpec((tm, tn), lambda i, j: (i, jnp.clip(j - n_a, 0, n_g - 1))),
                 pl.BlockSpec((tm, tn), lambda i, j: (i, jnp.clip(j - n_a - n_g, 0, n_g - 1)))]
    out_shape = [jax.ShapeDtypeStruct((m, N_SECTIONS_A * GROUP_W), F32),
                 jax.ShapeDtypeStruct((m, GROUP_W), F32),
                 jax.ShapeDtypeStruct((m, GROUP_W), F32)]
    if emit_w:
        out_specs.append(pl.BlockSpec((k, tn), lambda i, j: (0, j)))
        out_shape.append(jax.ShapeDtypeStruct((k, n), BF16))
    return pl.pallas_call(
        functools.partial(_in_proj_kernel, n_a=n_a, n_g=n_g),
        grid=(m // tm, n // tn),
        in_specs=[pl.BlockSpec((tm, k), lambda i, j: (i, 0)),
                  pl.BlockSpec((1, k), lambda i, j: (0, 0)),
                  pl.BlockSpec((k, tn), lambda i, j: (0, j))],
        out_specs=out_specs,
        out_shape=out_shape,
        scratch_shapes=[pltpu.VMEM((tm, k), BF16)],
        compiler_params=_cparams(("parallel", "arbitrary")),
        name=name,
    )(x, gain, w)


def _rotate(x, cos, sin_signed):
    lane = lax.broadcasted_iota(jnp.int32, x.shape, 1)
    nxt = pltpu.roll(x, HEAD_W - 1, 1)
    prv = pltpu.roll(x, 1, 1)
    return x * cos + jnp.where(lane % 2 == 0, nxt, prv) * sin_signed


def _retention_kernel(q_ref, k_ref, v_ref, g_ref, cos_ref, sin_ref, lg_ref, gn_ref, s0_ref,
                      o_ref, sf_ref, *, chunk, n_chunks, heads):
    rows_mm = RET_CHUNK
    ri = lax.broadcasted_iota(jnp.int32, (rows_mm, rows_mm), 0)
    ci = lax.broadcasted_iota(jnp.int32, (rows_mm, rows_mm), 1)
    rel = (ri - ci).astype(F32)
    idx = lax.broadcasted_iota(jnp.int32, (rows_mm, 1), 0).astype(F32)

    def pad(a):
        if chunk == rows_mm:
            return a
        return jnp.concatenate([a, jnp.zeros((rows_mm - chunk, a.shape[1]), a.dtype)], axis=0)

    for hh in range(heads):
        cols = slice(hh * HEAD_W, (hh + 1) * HEAD_W)
        lg = lg_ref[hh, 0:1, 0:1]
        intra = jnp.where(rel >= 0, jnp.exp(lg * jnp.maximum(rel, 0.0)), 0.0)
        q_dec = jnp.exp(lg * (idx + 1.0))
        k_dec = jnp.exp(lg * jnp.maximum(chunk - 1.0 - idx, 0.0))
        chunk_dec = jnp.exp(lg * float(chunk))
        gn = gn_ref[:, cols]

        def body(c, s, cols=cols, intra=intra, q_dec=q_dec, k_dec=k_dec, chunk_dec=chunk_dec, gn=gn):
            if n_chunks == 1:
                rows = slice(0, chunk)
            else:
                rows = pl.ds(pl.multiple_of(c * chunk, chunk), chunk)
            cos = cos_ref[rows, :]
            sin = sin_ref[rows, :]
            q = pad(_rotate(q_ref[rows, cols], cos, sin))
            k = pad(_rotate(k_ref[rows, cols], cos, sin) * (HEAD_W ** -0.5))
            qb = q.astype(BF16)
            vb = pad(v_ref[rows, cols]).astype(BF16)
            scores = _dot_nt(qb, k.astype(BF16)) * intra
            o = _dot(scores.astype(BF16), vb) + _dot(qb, s.astype(BF16)) * q_dec
            s_new = s * chunk_dec + _dot_tn((k * k_dec).astype(BF16), vb)
            gate = g_ref[rows, cols]
            y = _rms(o[0:chunk, :], gn)
            o_ref[rows, cols] = ((gate * jax.nn.sigmoid(gate)) * y).astype(o_ref.dtype)
            return s_new

        s0 = s0_ref[0, hh]
        if n_chunks == 1:
            s_fin = body(0, s0)
        else:
            s_fin = lax.fori_loop(0, n_chunks, body, s0, unroll=math.gcd(n_chunks, RET_UNROLL))
        sf_ref[0, hh] = s_fin


def _retention(proj_a, cos, sin, lg, gn, s0, *, batch, seq, heads, out_dtype, name):
    chunk = math.gcd(seq, RET_CHUNK)
    w = heads * HEAD_W
    n_hb = N_HEADS // heads

    def sec(s):
        return pl.BlockSpec((seq, w), lambda b, h, s=s: (b, s * n_hb + h))

    return pl.pallas_call(
        functools.partial(_retention_kernel, chunk=chunk, n_chunks=seq // chunk, heads=heads),
        grid=(batch, n_hb),
        in_specs=[sec(0), sec(1), sec(2), sec(3),
                  pl.BlockSpec((seq, HEAD_W), lambda b, h: (0, 0)),
                  pl.BlockSpec((seq, HEAD_W), lambda b, h: (0, 0)),
                  pl.BlockSpec((heads, 8, HEAD_W), lambda b, h: (h, 0, 0)),
                  pl.BlockSpec((1, w), lambda b, h: (0, h)),
                  pl.BlockSpec((1, heads, HEAD_W, HEAD_W), lambda b, h: (b, h, 0, 0))],
        out_specs=[pl.BlockSpec((seq, w), lambda b, h: (b, h)),
                   pl.BlockSpec((1, heads, HEAD_W, HEAD_W), lambda b, h: (b, h, 0, 0))],
        out_shape=[jax.ShapeDtypeStruct((batch * seq, GROUP_W), out_dtype),
                   jax.ShapeDtypeStruct((batch, N_HEADS, HEAD_W, HEAD_W), F32)],
        compiler_params=_cparams(("parallel", "parallel")),
        name=name,
    )(proj_a, proj_a, proj_a, proj_a, cos, sin, lg, gn, s0)


def _t5_bucket(qpos, kpos):
    n = jnp.maximum(qpos[:, None] - kpos[None, :], 0)
    nf = jnp.maximum(n, 1).astype(F32)
    large = MAX_EXACT + (jnp.log(nf / MAX_EXACT) / math.log(MAX_DISTANCE / MAX_EXACT)
                         * (N_BUCKETS - MAX_EXACT)).astype(jnp.int32)
    large = jnp.minimum(large, N_BUCKETS - 1)
    return jnp.where(n < MAX_EXACT, n, large)


def _bias_from_buckets(bkt, rb_ref, h):
    far = rb_ref[N_BUCKETS - 1, h]
    b = jnp.zeros(bkt.shape, F32)
    for u in range(N_BUCKETS - 1):
        b = jnp.where(bkt == u, rb_ref[u, h] - far, b)
    return b


def _lam(lamp_ref, lam_init):
    e1 = jnp.exp(jnp.sum(lamp_ref[0:1, :] * lamp_ref[1:2, :], axis=-1, keepdims=True))
    e2 = jnp.exp(jnp.sum(lamp_ref[2:3, :] * lamp_ref[3:4, :], axis=-1, keepdims=True))
    return e1 - e2 + lam_init


def _softmax_tile(s_ref, p_ref, m_ref, alpha_ref, add_fn, *, rb, keep_in_regs, cols=None):
    n_rows = s_ref.shape[0]
    col0, n_cols = (0, s_ref.shape[1]) if cols is None else cols
    n_slabs = n_cols // HEAD_W
    for r0 in range(0, n_rows, rb):
        r = slice(r0, r0 + rb)
        m_prev = m_ref[r, :]
        vals, part = [], None
        for j in range(n_slabs):
            c = slice(col0 + j * HEAD_W, col0 + (j + 1) * HEAD_W)
            s = s_ref[r, c]
            if add_fn is not None:
                s = s + add_fn(r0, j)
                if not keep_in_regs:
                    s_ref[r, c] = s
            if keep_in_regs:
                vals.append(s)
            part = s if part is None else jnp.maximum(part, s)
        m_new = jnp.maximum(m_prev, jnp.max(part, axis=-1, keepdims=True))
        for j in range(n_slabs):
            c = slice(col0 + j * HEAD_W, col0 + (j + 1) * HEAD_W)
            s = vals[j] if keep_in_regs else s_ref[r, c]
            p_ref[r, c] = jnp.exp2(s - m_new).astype(BF16)
        alpha_ref[r, :] = jnp.exp2(m_prev - m_new)
        m_ref[r, :] = m_new


def _accumulate(acc_ref, alpha_ref, p, v1):
    a = alpha_ref[...]
    acc_ref[...] = jnp.concatenate([a, a], axis=1) * acc_ref[...] + _dot(p, v1)


def _diff_prompt_kernel(q_ref, k_ref, v_ref, bkt_ref, rb_ref, lamp_ref, gn_ref, o_ref,
                        kbf, v1, qs, s_a, s_b, p_a, p_b, m_ref, alpha_a, alpha_b, acc_ref, bias_ref,
                        *, t, lam_init):
    h = pl.program_id(0)
    b = pl.program_id(1)
    qi = pl.program_id(2)

    @pl.when(qi == 0)
    def _():
        kbf[...] = k_ref[...].astype(BF16)
        v1[:, 0:HEAD_W] = v_ref[...].astype(BF16)
        v1[:, HEAD_W:2 * HEAD_W] = jnp.ones((v1.shape[0], HEAD_W), BF16)

    @pl.when(jnp.logical_and(b == 0, qi == 0))
    def _():
        ri = lax.broadcasted_iota(jnp.int32, (t, t), 0)
        ci = lax.broadcasted_iota(jnp.int32, (t, t), 1)
        bias_ref[0] = jnp.where(ci <= ri, 0.0, NEG_BIG)
        bias_ref[1] = jnp.zeros((t, t), F32)
        bias_ref[2] = jnp.zeros((t, t), F32)
        bri = lax.broadcasted_iota(jnp.int32, (BIAS_BAND, BIAS_BAND), 0)
        bci = lax.broadcasted_iota(jnp.int32, (BIAS_BAND, BIAS_BAND), 1)
        for rk in range(t // BIAS_BAND):
            r = slice(rk * BIAS_BAND, (rk + 1) * BIAS_BAND)
            bias_ref[0, r, r] = jnp.where(bci <= bri, _bias_from_buckets(bkt_ref[0, r, r], rb_ref, h) * LOG2E, NEG_BIG)
            if rk > 0:
                c = slice((rk - 1) * BIAS_BAND, rk * BIAS_BAND)
                bias_ref[0, r, c] = _bias_from_buckets(bkt_ref[0, r, c], rb_ref, h) * LOG2E
        r, c = slice(0, BIAS_BAND), slice(t - BIAS_BAND, t)
        bias_ref[1, r, c] = _bias_from_buckets(bkt_ref[1, r, c], rb_ref, h) * LOG2E

    q = q_ref[...] * (DIFF_D ** -0.5 * LOG2E)
    lane = lax.broadcasted_iota(jnp.int32, q.shape, 1)
    qs[0:t, :] = jnp.where(lane < DIFF_D, q, 0.0).astype(BF16)
    qs[t:2 * t, :] = jnp.where(lane >= DIFF_D, q, 0.0).astype(BF16)
    m_ref[...] = jnp.full(m_ref.shape, -jnp.inf, F32)
    acc_ref[...] = jnp.zeros(acc_ref.shape, F32)

    def scores(k, s_buf):
        rows = pl.ds(pl.multiple_of(jnp.minimum(k, qi) * t, t), t)
        s_buf[...] = _dot_nt(qs[...], kbf[rows, :])

    def update(k, s_buf, p_buf, alpha_buf):
        which = jnp.where(k == qi, 0, jnp.where(k == qi - 1, 1, 2))

        def add_fn(r0, j):
            q0 = r0 % t
            return bias_ref[which, q0:q0 + PROMPT_RB, j * HEAD_W:(j + 1) * HEAD_W]

        _softmax_tile(s_buf, p_buf, m_ref, alpha_buf, add_fn, rb=PROMPT_RB, keep_in_regs=True)
        rows = pl.ds(pl.multiple_of(k * t, t), t)
        _accumulate(acc_ref, alpha_buf, p_buf[...], v1[rows, :])

    n_tiles = qi + 1
    scores(0, s_a)

    def pair(j, carry):
        k0 = 2 * j
        scores(k0 + 1, s_b)
        update(k0, s_a, p_a, alpha_a)
        scores(k0 + 2, s_a)
        update(k0 + 1, s_b, p_b, alpha_b)
        return carry

    lax.fori_loop(0, n_tiles // 2, pair, 0)

    @pl.when(n_tiles % 2 == 1)
    def _():
        update(qi, s_a, p_a, alpha_a)

    lam = _lam(lamp_ref, lam_init)
    o = (acc_ref[0:t, 0:HEAD_W] / acc_ref[0:t, HEAD_W:2 * HEAD_W]
         - lam * (acc_ref[t:2 * t, 0:HEAD_W] / acc_ref[t:2 * t, HEAD_W:2 * HEAD_W]))
    o_ref[...] = (_rms(o, gn_ref[...]) * (1.0 - lam_init)).astype(o_ref.dtype)


def _diff_prompt(proj_a, dk, dv, rel_bias, lamp, gn, *, batch, seq, t, lam_init, name):
    nq = seq // t
    assert t % BIAS_BAND == 0 and BIAS_BAND >= MAX_DISTANCE
    pos = jnp.arange(t)
    bkt = jnp.stack([_t5_bucket(pos, pos), _t5_bucket(pos + t, pos)]).astype(jnp.int32)
    q_sec = (N_SECTIONS_A - 1) * N_HEADS
    return pl.pallas_call(
        functools.partial(_diff_prompt_kernel, t=t, lam_init=lam_init),
        grid=(N_HEADS, batch, nq),
        in_specs=[pl.BlockSpec((t, HEAD_W), lambda h, b, i: (b * nq + i, q_sec + h)),
                  pl.BlockSpec((seq, HEAD_W), lambda h, b, i: (b, h)),
                  pl.BlockSpec((seq, HEAD_W), lambda h, b, i: (b, h)),
                  pl.BlockSpec((2, t, t), lambda h, b, i: (0, 0, 0)),
                  pl.BlockSpec(memory_space=pltpu.SMEM),
                  pl.BlockSpec((4, DIFF_D), lambda h, b, i: (0, 0)),
                  pl.BlockSpec((1, HEAD_W), lambda h, b, i: (0, h))],
        out_specs=pl.BlockSpec((t, HEAD_W), lambda h, b, i: (b * nq + i, h)),
        out_shape=jax.ShapeDtypeStruct((batch * seq, GROUP_W), BF16),
        scratch_shapes=[pltpu.VMEM((seq, HEAD_W), BF16), pltpu.VMEM((seq, 2 * HEAD_W), BF16),
                        pltpu.VMEM((2 * t, HEAD_W), BF16),
                        pltpu.VMEM((2 * t, t), F32), pltpu.VMEM((2 * t, t), F32),
                        pltpu.VMEM((2 * t, t), BF16), pltpu.VMEM((2 * t, t), BF16),
                        pltpu.VMEM((2 * t, HEAD_W), F32), pltpu.VMEM((2 * t, HEAD_W), F32),
                        pltpu.VMEM((2 * t, HEAD_W), F32), pltpu.VMEM((2 * t, 2 * HEAD_W), F32),
                        pltpu.VMEM((3, t, t), F32)],
        compiler_params=_cparams(("arbitrary", "arbitrary", "arbitrary")),
        name=name,
    )(proj_a, dk, dv, bkt, rel_bias, lamp, gn)


def _diff_sample_kernel(pt_ref, q_ref, *refs, t_new, lam_init, n_pg):
    del pt_ref
    k_refs, v_refs = refs[:n_pg], refs[n_pg:2 * n_pg]
    (kn_ref, vn_ref, bkt_last_ref, bkt_new_ref, rb_ref, lamp_ref, gn_ref, o_ref,
     qf, qbf, v1all, knew, v1new, s_ref, p_ref, sn_ref, pn_ref, m_ref, alpha_ref, acc_ref,
     mask_ref, bias_last_ref, bias_new_ref) = refs[2 * n_pg:]
    b = pl.program_id(0)
    p = pl.program_id(1)
    last = pl.num_programs(1) - 1
    rows_per_head = 2 * t_new
    n_new = t_new * N_HEADS
    pg_rows = mask_ref.shape[1]

    def per_head_bias(bkt_ref):
        parts = []
        for h in range(N_HEADS):
            r0 = h * rows_per_head
            bk = bkt_ref[r0:r0 + rows_per_head, :]
            parts.append(jnp.where(bk >= 0, _bias_from_buckets(bk, rb_ref, h) * LOG2E, NEG_BIG))
        return jnp.concatenate(parts, axis=0)

    @pl.when(jnp.logical_and(b == 0, p == 0))
    def _():
        mask_ref[...] = jnp.where(bkt_last_ref[...] >= 0, 0.0, NEG_BIG)
        bias_last_ref[...] = per_head_bias(bkt_last_ref)
        bias_new_ref[...] = per_head_bias(bkt_new_ref)
        knew[...] = jnp.zeros(knew.shape, BF16)
        v1new[:, 0:HEAD_W] = jnp.zeros((v1new.shape[0], HEAD_W), BF16)
        v1new[:, HEAD_W:2 * HEAD_W] = jnp.ones((v1new.shape[0], HEAD_W), BF16)
        v1all[:, HEAD_W:2 * HEAD_W] = jnp.ones((v1all.shape[0], HEAD_W), BF16)

    @pl.when(p == 0)
    def _():
        q = q_ref[...] * (DIFF_D ** -0.5 * LOG2E)
        lane = lax.broadcasted_iota(jnp.int32, (t_new, HEAD_W), 1)
        for h in range(N_HEADS):
            cols = slice(h * HEAD_W, (h + 1) * HEAD_W)
            r0 = h * rows_per_head
            qf[r0:r0 + t_new, :] = jnp.where(lane < DIFF_D, q[:, cols], 0.0)
            qf[r0 + t_new:r0 + 2 * t_new, :] = jnp.where(lane >= DIFF_D, q[:, cols], 0.0)
        qbf[...] = qf[...].astype(BF16)
        m_ref[...] = jnp.full(m_ref.shape, -jnp.inf, F32)
        acc_ref[...] = jnp.zeros(acc_ref.shape, F32)

    def add_mask(r0, j):
        return mask_ref[r0:r0 + SAMPLE_RB, j * HEAD_W:(j + 1) * HEAD_W]

    def add_last(r0, j):
        return bias_last_ref[r0:r0 + SAMPLE_RB, j * HEAD_W:(j + 1) * HEAD_W]

    def page_update(g, add_fn):
        pg = slice(g * pg_rows, (g + 1) * pg_rows)
        _softmax_tile(s_ref, p_ref, m_ref, alpha_ref.at[g], add_fn, rb=SAMPLE_RB, keep_in_regs=False,
                      cols=(g * pg_rows, pg_rows))
        _accumulate(acc_ref, alpha_ref.at[g], p_ref[:, pg], v1all[pg, :])

    def page_scores(g):
        pg = slice(g * pg_rows, (g + 1) * pg_rows)
        v1all[pg, 0:HEAD_W] = v_refs[g][0].astype(BF16)
        s_ref[:, pg] = _dot_nt(qbf[...], k_refs[g][0].astype(BF16))

    page_scores(0)
    for g in range(n_pg - 1):
        page_scores(g + 1)
        page_update(g, add_mask)

    @pl.when(p < last)
    def _():
        page_update(n_pg - 1, add_mask)

    @pl.when(p == last)
    def _():
        page_update(n_pg - 1, add_last)
        knew[0:n_new, :] = kn_ref[0].astype(BF16)
        v1new[0:n_new, 0:HEAD_W] = vn_ref[0].astype(BF16)
        sn_ref[...] = _dot_nt(qbf[...], knew[...])
        _softmax_tile(sn_ref, pn_ref, m_ref, alpha_ref.at[0],
                      lambda r0, j: bias_new_ref[r0:r0 + SAMPLE_RB, :], rb=SAMPLE_RB, keep_in_regs=True)
        _accumulate(acc_ref, alpha_ref.at[0], pn_ref[...], v1new[...])

        lam = _lam(lamp_ref, lam_init)
        for h in range(N_HEADS):
            cols = slice(h * HEAD_W, (h + 1) * HEAD_W)
            r0 = h * rows_per_head
            r1 = r0 + t_new
            o = (acc_ref[r0:r1, 0:HEAD_W] / acc_ref[r0:r1, HEAD_W:2 * HEAD_W]
                 - lam * (acc_ref[r1:r1 + t_new, 0:HEAD_W] / acc_ref[r1:r1 + t_new, HEAD_W:2 * HEAD_W]))
            o_ref[:, cols] = (_rms(o, gn_ref[:, cols]) * (1.0 - lam_init)).astype(o_ref.dtype)


def _diff_sample(proj_a, dk, dv, cache_k, cache_v, page_table, rel_bias, lamp, gn, *, t_new, lam_init, name):
    batch, n_pages = page_table.shape
    n_phys, page = cache_k.shape[0], cache_k.shape[1]
    past = n_pages * page
    rows = N_HEADS * 2 * t_new
    kv_rows = page * N_HEADS
    new_rows = 128
    assert t_new * N_HEADS <= new_rows
    q_head = jnp.repeat(jnp.arange(N_HEADS), 2 * t_new)
    q_pos = past + jnp.tile(jnp.arange(t_new), 2 * N_HEADS)

    def buckets(n_cols, first_key_pos, n_valid_cols):
        col = jnp.arange(n_cols)
        k_pos = first_key_pos + col // N_HEADS
        ok = ((col % N_HEADS)[None, :] == q_head[:, None]) & (k_pos[None, :] <= q_pos[:, None])
        ok = ok & (col < n_valid_cols)[None, :]
        return jnp.where(ok, _t5_bucket(q_pos, k_pos), -1).astype(jnp.int32)

    bkt_last = buckets(kv_rows, past - page, kv_rows)
    bkt_new = buckets(new_rows, past, t_new * N_HEADS)
    n_pg = math.gcd(n_pages, PAGES_PER_STEP)

    def page_spec(g):
        return pl.BlockSpec((1, kv_rows, HEAD_W), lambda b, p, pt, g=g: (pt[b, p * n_pg + g], 0, 0))

    grid_spec = pltpu.PrefetchScalarGridSpec(
        num_scalar_prefetch=1,
        grid=(batch, n_pages // n_pg),
        in_specs=([pl.BlockSpec((t_new, GROUP_W), lambda b, p, pt: (b, N_SECTIONS_A - 1))]
                  + [page_spec(g) for g in range(n_pg)] + [page_spec(g) for g in range(n_pg)]
                  + [pl.BlockSpec((1, t_new * N_HEADS, HEAD_W), lambda b, p, pt: (b, 0, 0)),
                     pl.BlockSpec((1, t_new * N_HEADS, HEAD_W), lambda b, p, pt: (b, 0, 0)),
                     pl.BlockSpec((rows, kv_rows), lambda b, p, pt: (0, 0)),
                     pl.BlockSpec((rows, new_rows), lambda b, p, pt: (0, 0)),
                     pl.BlockSpec(memory_space=pltpu.SMEM),
                     pl.BlockSpec((4, DIFF_D), lambda b, p, pt: (0, 0)),
                     pl.BlockSpec((1, GROUP_W), lambda b, p, pt: (0, 0))]),
        out_specs=pl.BlockSpec((t_new, GROUP_W), lambda b, p, pt: (b, 0)),
        scratch_shapes=[pltpu.VMEM((rows, HEAD_W), F32), pltpu.VMEM((rows, HEAD_W), BF16),
                        pltpu.VMEM((n_pg * kv_rows, 2 * HEAD_W), BF16),
                        pltpu.VMEM((new_rows, HEAD_W), BF16), pltpu.VMEM((new_rows, 2 * HEAD_W), BF16),
                        pltpu.VMEM((rows, n_pg * kv_rows), F32), pltpu.VMEM((rows, n_pg * kv_rows), BF16),
                        pltpu.VMEM((rows, new_rows), F32), pltpu.VMEM((rows, new_rows), BF16),
                        pltpu.VMEM((rows, HEAD_W), F32), pltpu.VMEM((n_pg, rows, HEAD_W), F32),
                        pltpu.VMEM((rows, 2 * HEAD_W), F32),
                        pltpu.VMEM((rows, kv_rows), F32), pltpu.VMEM((rows, kv_rows), F32),
                        pltpu.VMEM((rows, new_rows), F32)],
    )
    ck = cache_k.reshape(n_phys, kv_rows, HEAD_W)
    cv = cache_v.reshape(n_phys, kv_rows, HEAD_W)
    return pl.pallas_call(
        functools.partial(_diff_sample_kernel, t_new=t_new, lam_init=lam_init, n_pg=n_pg),
        grid_spec=grid_spec,
        out_shape=jax.ShapeDtypeStruct((batch * t_new, GROUP_W), F32),
        compiler_params=_cparams(("arbitrary", "arbitrary")),
        name=name,
    )(page_table, proj_a, *([ck] * n_pg), *([cv] * n_pg),
      dk.reshape(batch, t_new * N_HEADS, HEAD_W), dv.reshape(batch, t_new * N_HEADS, HEAD_W),
      bkt_last, bkt_new, rel_bias, lamp, gn)


def _mix_out_kernel(ret_ref, diff_ref, w_ret_ref, w_diff_ref, gain_ref, x_ref, gain_q_ref, wq_ref, *refs, has_mem):
    if has_mem:
        mk_ref, mv_ref, x1_ref, o_ref = refs
    else:
        x1_ref, o_ref = refs
    y = _dot(ret_ref[...].astype(BF16), w_ret_ref[...]) + _dot(diff_ref[...].astype(BF16), w_diff_ref[...])
    x1 = x_ref[...] + _rms(y, gain_ref[...])
    x1_ref[...] = x1
    q = _dot(_rms(x1, gain_q_ref[...]).astype(BF16), wq_ref[...])
    if not has_mem:
        o_ref[...] = q.astype(o_ref.dtype)
        return
    for h in range(N_XHEADS):
        cols = slice(h * HEAD_W, (h + 1) * HEAD_W)
        s = _dot_nt(q[:, cols].astype(BF16), mk_ref[0, :, cols].astype(BF16)) * (HEAD_W ** -0.5)
        e = jnp.exp(s - jnp.max(s, axis=-1, keepdims=True))
        p = e / jnp.sum(e, axis=-1, keepdims=True)
        o_ref[:, cols] = _dot(p.astype(BF16), mv_ref[0, :, cols].astype(BF16)).astype(o_ref.dtype)


def _mix_out(ret_o, diff_o, w_out, gain, x, gain_q, w_xq, mem, *, tm, rows_per_batch, out_dtype, name):
    m, n = x.shape
    ka = ret_o.shape[1]
    xw = w_xq.shape[1]
    in_specs = [pl.BlockSpec((tm, ka), lambda i: (i, 0)), pl.BlockSpec((tm, ka), lambda i: (i, 0)),
                pl.BlockSpec((ka, n), lambda i: (0, 0)), pl.BlockSpec((ka, n), lambda i: (1, 0)),
                pl.BlockSpec((1, n), lambda i: (0, 0)), pl.BlockSpec((tm, n), lambda i: (i, 0)),
                pl.BlockSpec((1, n), lambda i: (0, 0)), pl.BlockSpec((n, xw), lambda i: (0, 0))]
    args = [ret_o, diff_o, w_out, w_out, gain, x, gain_q, w_xq]
    if mem is not None:
        assert rows_per_batch % tm == 0
        per = rows_per_batch // tm
        n_mem = mem[0].shape[1]
        in_specs += [pl.BlockSpec((1, n_mem, xw), lambda i: (i // per, 0, 0))] * 2
        args += list(mem)
    return pl.pallas_call(
        functools.partial(_mix_out_kernel, has_mem=mem is not None),
        grid=(m // tm,),
        in_specs=in_specs,
        out_specs=[pl.BlockSpec((tm, n), lambda i: (i, 0)), pl.BlockSpec((tm, xw), lambda i: (i, 0))],
        out_shape=[jax.ShapeDtypeStruct((m, n), F32), jax.ShapeDtypeStruct((m, xw), out_dtype)],
        compiler_params=_cparams(("parallel",)),
        name=name,
    )(*args)


def _cross_interleaved_kernel(q_ref, mk_ref, mv_ref, o_ref):
    tq = q_ref.shape[1]
    q = q_ref[0]
    qs = jnp.concatenate([q[:, h * HEAD_W:(h + 1) * HEAD_W] for h in range(N_XHEADS)], axis=0).astype(BF16)
    s = _dot_nt(qs, mk_ref[0].astype(BF16)) * (HEAD_W ** -0.5)
    row_head = lax.broadcasted_iota(jnp.int32, s.shape, 0) // tq
    col_head = lax.broadcasted_iota(jnp.int32, s.shape, 1) % N_XHEADS
    s = jnp.where(row_head == col_head, s, NEG_BIG)
    e = jnp.exp(s - jnp.max(s, axis=-1, keepdims=True))
    p = e / jnp.sum(e, axis=-1, keepdims=True)
    o = _dot(p.astype(BF16), mv_ref[0].astype(BF16))
    for h in range(N_XHEADS):
        o_ref[0, :, h * HEAD_W:(h + 1) * HEAD_W] = o[h * tq:(h + 1) * tq, :].astype(o_ref.dtype)


def _cross_attend_interleaved(q, mk, mv, *, name):
    b, t, w = q.shape
    rows = mk.shape[1]
    return pl.pallas_call(
        _cross_interleaved_kernel,
        grid=(b,),
        in_specs=[pl.BlockSpec((1, t, w), lambda i: (i, 0, 0)),
                  pl.BlockSpec((1, rows, HEAD_W), lambda i: (i, 0, 0)),
                  pl.BlockSpec((1, rows, HEAD_W), lambda i: (i, 0, 0))],
        out_specs=pl.BlockSpec((1, t, w), lambda i: (i, 0, 0)),
        out_shape=jax.ShapeDtypeStruct((b, t, w), q.dtype),
        compiler_params=_cparams(("parallel",)),
        name=name,
    )(q, mk, mv)


def _xo_ffn_kernel(x_ref, o_ref, wxo_ref, gx_ref, gpre_ref, wg_ref, wu_ref, wd_ref, gpost_ref, y_ref, *rest):
    x2_ref, h_ref, acc_ref = rest[-3:]
    copies = rest[:-3]
    f = pl.program_id(1)

    def weight(w_ref, i):
        w = w_ref[...]
        if copies:
            w = w.astype(BF16)
            copies[i][...] = w
        return w

    @pl.when(f == 0)
    def _():
        x2 = x_ref[...] + _rms(_dot(o_ref[...].astype(BF16), weight(wxo_ref, 0)), gx_ref[...])
        x2_ref[...] = x2
        h_ref[...] = _rms(x2, gpre_ref[...]).astype(BF16)
        acc_ref[...] = jnp.zeros(acc_ref.shape, F32)

    h = h_ref[...]
    g = _dot(h, weight(wg_ref, 1))
    a = (g * jax.nn.sigmoid(g)) * _dot(h, weight(wu_ref, 2))
    acc_ref[...] += _dot(a.astype(BF16), weight(wd_ref, 3))

    @pl.when(f == pl.num_programs(1) - 1)
    def _():
        y_ref[...] = x2_ref[...] + _rms(acc_ref[...], gpost_ref[...])


def _xo_ffn(x, o, w_xo, gx, gpre, wg, wu, wd, gpost, *, tm, tf, name):
    m, d = x.shape
    xw = o.shape[1]
    dff = wg.shape[1]
    emit_w = wg.dtype != BF16
    assert not emit_w or m == tm, "each weight tile must be visited once to be copied out"
    w_specs = [pl.BlockSpec((xw, d), lambda i, f: (0, 0)),
               pl.BlockSpec((d, tf), lambda i, f: (0, f)),
               pl.BlockSpec((d, tf), lambda i, f: (0, f)),
               pl.BlockSpec((tf, d), lambda i, f: (f, 0))]
    out_specs = [pl.BlockSpec((tm, d), lambda i, f: (i, 0))]
    out_shape = [jax.ShapeDtypeStruct((m, d), F32)]
    if emit_w:
        out_specs += w_specs
        out_shape += [jax.ShapeDtypeStruct(a.shape, BF16) for a in (w_xo, wg, wu, wd)]
    out = pl.pallas_call(
        _xo_ffn_kernel,
        grid=(m // tm, dff // tf),
        in_specs=[pl.BlockSpec((tm, d), lambda i, f: (i, 0)),
                  pl.BlockSpec((tm, xw), lambda i, f: (i, 0)),
                  w_specs[0],
                  pl.BlockSpec((1, d), lambda i, f: (0, 0)),
                  pl.BlockSpec((1, d), lambda i, f: (0, 0)),
                  w_specs[1], w_specs[2], w_specs[3],
                  pl.BlockSpec((1, d), lambda i, f: (0, 0))],
        out_specs=out_specs,
        out_shape=out_shape,
        scratch_shapes=[pltpu.VMEM((tm, d), F32), pltpu.VMEM((tm, d), BF16), pltpu.VMEM((tm, d), F32)],
        compiler_params=_cparams(("parallel", "arbitrary")),
        name=name,
    )(x, o, w_xo, gx, gpre, wg, wu, wd, gpost)
    return out if emit_w else out[0]


def _rope_tables(pos):
    inv_freq = ROPE_BASE ** (-jnp.arange(0, HEAD_W, 2, dtype=F32) / HEAD_W)
    ang = pos.astype(F32)[:, None] * inv_freq[None, :]
    cos = jnp.repeat(jnp.cos(ang), 2, axis=-1)
    sin = jnp.sin(ang)
    return cos, jnp.stack([-sin, sin], axis=-1).reshape(cos.shape)


def _layer(x, pos, s0, mem_k, mem_v, paged, w, *, batch, seq, lam_init, tag):
    m = batch * seq
    big = paged is None
    tm = 512 if big else m
    w = dict(w)
    proj_a, dk, dv, *w_in_copy = _in_proj(x, w["n_pre_mix"], w["w_in"], tm=1024 if big else m, tn=512,
                                          name=f"in_proj_{tag}")
    if w_in_copy:
        w["w_in"] = w_in_copy[0]
    cos, sin = _rope_tables(pos)
    ret_o, ret_s = _retention(proj_a, cos, sin, w["lg"], w["ret_gn"], s0, batch=batch, seq=seq,
                              heads=1 if big else N_HEADS, out_dtype=BF16 if big else F32, name=f"retention_{tag}")
    if big:
        diff_o = _diff_prompt(proj_a, dk, dv, w["rel_bias"], w["lamp"], w["diff_gn"], batch=batch, seq=seq,
                              t=512, lam_init=lam_init, name=f"diff_attn_{tag}")
    else:
        diff_o = _diff_sample(proj_a, dk, dv, *paged, w["rel_bias"], w["lamp"], w["diff_gn"], t_new=seq,
                              lam_init=lam_init, name=f"diff_attn_{tag}")
    xw = N_XHEADS * HEAD_W
    x, o = _mix_out(ret_o, diff_o, w["w_out"], w["n_post_mix"], x, w["n_pre_x"], w["w_xq"],
                    (mem_k, mem_v) if big else None, tm=tm, rows_per_batch=seq,
                    out_dtype=BF16 if big else F32, name=f"mix_out_{tag}")
    if not big:
        o = _cross_attend_interleaved(o.reshape(batch, seq, xw), mem_k, mem_v, name=f"cross_{tag}").reshape(m, xw)
    out = _xo_ffn(x, o, w["w_xo"], w["n_post_x"], w["n_pre_ffn"], w["w_gate"], w["w_up"], w["w_down"],
                  w["n_post_ffn"], tm=tm, tf=512 if big else 256, name=f"ffn_{tag}")
    if isinstance(out, (list, tuple)):
        x, w["w_xo"], w["w_gate"], w["w_up"], w["w_down"] = out
    else:
        x = out
    return x, dk, dv, ret_s, w


def kernel(x_prompt, x_sample, cache_k, cache_v, state_ret, cache_mem_k, cache_mem_v, page_table, mem_prompt,
           rel_bias, norm_pre_mix, norm_post_mix, norm_pre_x, norm_post_x, norm_pre_ffn, norm_post_ffn, norm_mem,
           w_in, w_out, ret_gn, diff_gn, lam_q1, lam_k1, lam_q2, lam_k2,
           w_xq, w_xk, w_xv, w_xo, w_gate, w_up, w_down):
    b_p, t_p, d = x_prompt.shape
    b_s, t_s, _ = x_sample.shape
    depth = w_in.shape[0]
    n_mem = mem_prompt.shape[1]
    xw = N_XHEADS * HEAD_W
    past_len = page_table.shape[1] * cache_k.shape[2]
    pos_p = jnp.arange(t_p)
    pos_s = past_len + jnp.arange(t_s)
    log_g = jnp.log1p(-(2.0 ** (-5.0 - jnp.arange(N_HEADS, dtype=F32))))
    lg = jnp.broadcast_to(log_g[:, None, None], (N_HEADS, 8, HEAD_W))
    xp = x_prompt.reshape(b_p * t_p, d)
    xs = x_sample.reshape(b_s * t_s, d)
    outs = [[] for _ in range(8)]
    for l in range(depth):
        lam_init = 0.8 - 0.6 * math.exp(-0.3 * l)
        w = {
            "n_pre_mix": norm_pre_mix[l][None], "n_post_mix": norm_post_mix[l][None],
            "n_pre_x": norm_pre_x[l][None], "n_post_x": norm_post_x[l][None],
            "n_pre_ffn": norm_pre_ffn[l][None], "n_post_ffn": norm_post_ffn[l][None],
            "w_in": w_in[l], "w_out": w_out[l].astype(BF16),
            "ret_gn": ret_gn[l][None], "diff_gn": diff_gn[l][None],
            "lamp": jnp.stack([lam_q1[l], lam_k1[l], lam_q2[l], lam_k2[l]]),
            "w_xq": w_xq[l].astype(BF16), "w_xo": w_xo[l],
            "w_gate": w_gate[l], "w_up": w_up[l], "w_down": w_down[l],
            "rel_bias": rel_bias, "lg": lg,
        }
        mem_flat = mem_prompt.reshape(b_p * n_mem, d)
        mk_p = _norm_matmul(mem_flat, norm_mem[l][None], w_xk[l], tm=b_p * n_mem, tn=xw, out_dtype=F32, name="mem_k")
        mv_p = _norm_matmul(mem_flat, norm_mem[l][None], w_xv[l], tm=b_p * n_mem, tn=xw, out_dtype=F32, name="mem_v")
        s0 = jnp.zeros((b_p, N_HEADS, HEAD_W, HEAD_W), F32)
        xs, dk_s, dv_s, s_s, w = _layer(xs, pos_s, state_ret[l],
                                        cache_mem_k[l].reshape(b_s, n_mem * N_XHEADS, HEAD_W),
                                        cache_mem_v[l].reshape(b_s, n_mem * N_XHEADS, HEAD_W),
                                        (cache_k[l], cache_v[l], page_table), w,
                                        batch=b_s, seq=t_s, lam_init=lam_init, tag="sample")
        xp, dk_p, dv_p, s_p, _ = _layer(xp, pos_p, s0, mk_p.reshape(b_p, n_mem, xw), mv_p.reshape(b_p, n_mem, xw),
                                        None, w, batch=b_p, seq=t_p, lam_init=lam_init, tag="prompt")
        per_layer = (dk_p.reshape(b_p, t_p, N_HEADS, HEAD_W), dv_p.reshape(b_p, t_p, N_HEADS, HEAD_W), s_p,
                     mk_p.reshape(b_p, n_mem, N_XHEADS, HEAD_W), mv_p.reshape(b_p, n_mem, N_XHEADS, HEAD_W),
                     dk_s.reshape(b_s, t_s, N_HEADS, HEAD_W), dv_s.reshape(b_s, t_s, N_HEADS, HEAD_W), s_s)
        for acc, val in zip(outs, per_layer):
            acc.append(val)
    stacked = [jnp.stack(o) for o in outs]
    return (xp.reshape(b_p, t_p, d), xs.reshape(b_s, t_s, d), *stacked)
```

```python
import functools
import math

import jax
import jax.numpy as jnp
import numpy as np
from jax import lax
from jax.experimental import pallas as pl
from jax.experimental.pallas import tpu as pltpu

F32 = jnp.float32
BF16 = jnp.bfloat16

D_MODEL = 2048
N_HEADS = 8
HEAD_W = 128
GROUP_W = N_HEADS * HEAD_W
DIFF_D = 64
ROPE_BASE = 10000.0
RET_CHUNK = 128
N_BUCKETS = 32
MAX_EXACT = N_BUCKETS // 2
MAX_DISTANCE = 128
BIAS_BAND = 128
N_XHEADS = 4
RMS_EPS = 1e-6
NEG_BIG = -1e30
PROMPT_RB = 32
SAMPLE_RB = 16
PAGES_PER_STEP = 8
PAGE_SLOTS = 4
RET_UNROLL = 8
N_SECTIONS_A = 5
VMEM_LIMIT = 56 * 1024 * 1024


def _cparams(sem):
    return pltpu.CompilerParams(dimension_semantics=sem, vmem_limit_bytes=VMEM_LIMIT)


def _rms(x, gain):
    return x * lax.rsqrt(jnp.mean(x * x, axis=-1, keepdims=True) + RMS_EPS) * gain


def _dot(a, b):
    return jnp.dot(a, b, preferred_element_type=F32)


def _dot_nt(a, b):
    return lax.dot_general(a, b, (((1,), (1,)), ((), ())), preferred_element_type=F32)


def _dot_tn(a, b):
    return lax.dot_general(a, b, (((0,), (0,)), ((), ())), preferred_element_type=F32)


def _norm_matmul_pair_kernel(x_ref, g_ref, wa_ref, wb_ref, oa_ref, ob_ref):
    h = _rms(x_ref[...], g_ref[...]).astype(BF16)
    oa_ref[...] = _dot(h, wa_ref[...].astype(BF16))
    ob_ref[...] = _dot(h, wb_ref[...].astype(BF16))


def _norm_matmul_pair(x, gain, wa, wb, *, name):
    m, k = x.shape
    n = wa.shape[1]
    full = lambda shape: pl.BlockSpec(shape, lambda i: (0, 0))
    return pl.pallas_call(
        _norm_matmul_pair_kernel,
        grid=(1,),
        in_specs=[full((m, k)), full((1, k)), full((k, n)), full((k, n))],
        out_specs=[full((m, n)), full((m, n))],
        out_shape=[jax.ShapeDtypeStruct((m, n), F32)] * 2,
        compiler_params=_cparams(("arbitrary",)),
        name=name,
    )(x, gain, wa, wb)


def _in_proj_kernel(x_ref, g_ref, w_ref, a_ref, dk_ref, dv_ref, *rest, n_a, n_g):
    j = pl.program_id(1)
    h_ref = rest[-1]

    @pl.when(j == 0)
    def _():
        h_ref[...] = _rms(x_ref[...], g_ref[...]).astype(BF16)

    w = w_ref[...]
    if len(rest) == 2:
        w = w.astype(BF16)
        rest[0][...] = w
    res = _dot(h_ref[...], w)

    @pl.when(j < n_a)
    def _():
        a_ref[...] = res

    @pl.when(jnp.logical_and(j >= n_a, j < n_a + n_g))
    def _():
        dk_ref[...] = res

    @pl.when(j >= n_a + n_g)
    def _():
        dv_ref[...] = res


def _in_proj(x, gain, w, *, tm, tn, name):
    m, k = x.shape
    n = w.shape[1]
    n_g = GROUP_W // tn
    n_a = N_SECTIONS_A * n_g
    emit_w = w.dtype != BF16
    assert not emit_w or m == tm, "each weight tile must be visited once to be copied out"
    out_specs = [pl.BlockSpec((tm, tn), lambda i, j: (i, jnp.minimum(j, n_a - 1))),
                 pl.BlockSpec((tm, tn), lambda i, j: (i, jnp.clip(j - n_a, 0, n_g - 1))),
                 pl.BlockSpec((tm, tn), lambda i, j: (i, jnp.clip(j - n_a - n_g, 0, n_g - 1)))]
    out_shape = [jax.ShapeDtypeStruct((m, N_SECTIONS_A * GROUP_W), F32),
                 jax.ShapeDtypeStruct((m, GROUP_W), F32),
                 jax.ShapeDtypeStruct((m, GROUP_W), F32)]
    if emit_w:
        out_specs.append(pl.BlockSpec((k, tn), lambda i, j: (0, j)))
        out_shape.append(jax.ShapeDtypeStruct((k, n), BF16))
    return pl.pallas_call(
        functools.partial(_in_proj_kernel, n_a=n_a, n_g=n_g),
        grid=(m // tm, n // tn),
        in_specs=[pl.BlockSpec((tm, k), lambda i, j: (i, 0)),
                  pl.BlockSpec((1, k), lambda i, j: (0, 0)),
                  pl.BlockSpec((k, tn), lambda i, j: (0, j))],
        out_specs=out_specs,
        out_shape=out_shape,
        scratch_shapes=[pltpu.VMEM((tm, k), BF16)],
        compiler_params=_cparams(("parallel", "arbitrary")),
        name=name,
    )(x, gain, w)


def _rotate(x, cos, sin_signed):
    lane = lax.broadcasted_iota(jnp.int32, x.shape, 1)
    nxt = pltpu.roll(x, HEAD_W - 1, 1)
    prv = pltpu.roll(x, 1, 1)
    return x * cos + jnp.where(lane % 2 == 0, nxt, prv) * sin_signed


def _retention_kernel(q_ref, k_ref, v_ref, g_ref, cos_ref, sin_ref, lg_ref, gn_ref, s0_ref,
                      o_ref, sf_ref, kr_ref, st_ref, *, chunk, n_chunks, heads):
    rows_mm = RET_CHUNK
    ri = lax.broadcasted_iota(jnp.int32, (rows_mm, rows_mm), 0)
    ci = lax.broadcasted_iota(jnp.int32, (rows_mm, rows_mm), 1)
    rel = (ri - ci).astype(F32)
    idx = lax.broadcasted_iota(jnp.int32, (rows_mm, 1), 0).astype(F32)

    def pad(a):
        if chunk == rows_mm:
            return a
        return jnp.concatenate([a, jnp.zeros((rows_mm - chunk, a.shape[1]), a.dtype)], axis=0)

    for hh in range(heads):
        cols = slice(hh * HEAD_W, (hh + 1) * HEAD_W)
        lg = lg_ref[hh, 0:1, 0:1]
        intra = jnp.where(rel >= 0, jnp.exp(lg * jnp.maximum(rel, 0.0)), 0.0)
        q_dec = jnp.exp(lg * (idx + 1.0))
        k_dec = jnp.exp(lg * jnp.maximum(chunk - 1.0 - idx, 0.0))
        chunk_dec = jnp.exp(lg * float(chunk))
        gn = gn_ref[:, cols]

        def chunk_rows(c):
            if n_chunks == 1:
                return slice(0, chunk), slice(0, rows_mm)
            return (pl.ds(pl.multiple_of(c * chunk, chunk), chunk),
                    pl.ds(pl.multiple_of(c * rows_mm, rows_mm), rows_mm))

        def state_step(c, s, hh=hh, cols=cols, k_dec=k_dec, chunk_dec=chunk_dec):
            rows, rows_p = chunk_rows(c)
            k = pad(_rotate(k_ref[rows, cols], cos_ref[rows, :], sin_ref[rows, :]) * (HEAD_W ** -0.5))
            kr_ref[rows_p, cols] = k.astype(BF16)
            st_ref[hh, c] = s.astype(BF16)
            vb = pad(v_ref[rows, cols]).astype(BF16)
            return s * chunk_dec + _dot_tn((k * k_dec).astype(BF16), vb)

        def output_step(c, carry, hh=hh, cols=cols, intra=intra, q_dec=q_dec, gn=gn):
            rows, rows_p = chunk_rows(c)
            qb = _rotate(q_ref[rows, cols], cos_ref[rows, :], sin_ref[rows, :]).astype(BF16)
            vb = pad(v_ref[rows, cols]).astype(BF16)
            scores = _dot_nt(qb, kr_ref[rows_p, cols]) * intra[0:chunk, :]
            o = _dot(scores.astype(BF16), vb) + _dot(qb, st_ref[hh, c]) * q_dec[0:chunk, :]
            gate = g_ref[rows, cols]
            y = _rms(o, gn)
            o_ref[rows, cols] = ((gate * jax.nn.sigmoid(gate)) * y).astype(o_ref.dtype)
            return carry

        s0 = s0_ref[0, hh]
        if n_chunks == 1:
            sf_ref[0, hh] = state_step(0, s0)
            output_step(0, 0)
        else:
            unroll = math.gcd(n_chunks, RET_UNROLL)
            sf_ref[0, hh] = lax.fori_loop(0, n_chunks, state_step, s0, unroll=unroll)
            lax.fori_loop(0, n_chunks, output_step, 0, unroll=unroll)


def _retention(proj_a, cos, sin, lg, gn, s0, *, batch, seq, heads, out_dtype, name):
    chunk = math.gcd(seq, RET_CHUNK)
    w = heads * HEAD_W
    n_hb = N_HEADS // heads

    def sec(s):
        return pl.BlockSpec((seq, w), lambda b, h, s=s: (b, s * n_hb + h))

    return pl.pallas_call(
        functools.partial(_retention_kernel, chunk=chunk, n_chunks=seq // chunk, heads=heads),
        grid=(batch, n_hb),
        in_specs=[sec(0), sec(1), sec(2), sec(3),
                  pl.BlockSpec((seq, HEAD_W), lambda b, h: (0, 0)),
                  pl.BlockSpec((seq, HEAD_W), lambda b, h: (0, 0)),
                  pl.BlockSpec((heads, 8, HEAD_W), lambda b, h: (h, 0, 0)),
                  pl.BlockSpec((1, w), lambda b, h: (0, h)),
                  pl.BlockSpec((1, heads, HEAD_W, HEAD_W), lambda b, h: (b, h, 0, 0))],
        out_specs=[pl.BlockSpec((seq, w), lambda b, h: (b, h)),
                   pl.BlockSpec((1, heads, HEAD_W, HEAD_W), lambda b, h: (b, h, 0, 0))],
        out_shape=[jax.ShapeDtypeStruct((batch * seq, GROUP_W), out_dtype),
                   jax.ShapeDtypeStruct((batch, N_HEADS, HEAD_W, HEAD_W), F32)],
        scratch_shapes=[pltpu.VMEM((seq // chunk * RET_CHUNK, w), BF16),
                        pltpu.VMEM((heads, seq // chunk, HEAD_W, HEAD_W), BF16)],
        compiler_params=_cparams(("parallel", "parallel")),
        name=name,
    )(proj_a, proj_a, proj_a, proj_a, cos, sin, lg, gn, s0)


def _t5_bucket(qpos, kpos):
    n = np.maximum(qpos[:, None] - kpos[None, :], 0)
    nf = np.maximum(n, 1).astype(np.float32)
    large = MAX_EXACT + (np.log(nf / np.float32(MAX_EXACT)) / np.float32(math.log(MAX_DISTANCE / MAX_EXACT))
                         * np.float32(N_BUCKETS - MAX_EXACT)).astype(np.int32)
    large = np.minimum(large, N_BUCKETS - 1)
    return np.where(n < MAX_EXACT, n, large).astype(np.int32)


def _bias_from_buckets(bkt, rb_ref, h):
    far = rb_ref[N_BUCKETS - 1, h]
    b = jnp.zeros(bkt.shape, F32)
    for u in range(N_BUCKETS - 1):
        b = jnp.where(bkt == u, rb_ref[u, h] - far, b)
    return b


def _lam(lamp_ref, lam_init):
    e1 = jnp.exp(jnp.sum(lamp_ref[0:1, :] * lamp_ref[1:2, :], axis=-1, keepdims=True))
    e2 = jnp.exp(jnp.sum(lamp_ref[2:3, :] * lamp_ref[3:4, :], axis=-1, keepdims=True))
    return e1 - e2 + lam_init


def _softmax_tile(s_ref, p_ref, m_ref, alpha_ref, add_fn, *, rb, keep_in_regs, cols=None):
    n_rows = s_ref.shape[0]
    col0, n_cols = (0, s_ref.shape[1]) if cols is None else cols
    n_slabs = n_cols // HEAD_W
    for r0 in range(0, n_rows, rb):
        r = slice(r0, r0 + rb)
        m_prev = m_ref[r, :]
        vals, part = [], None
        for j in range(n_slabs):
            c = slice(col0 + j * HEAD_W, col0 + (j + 1) * HEAD_W)
            s = s_ref[r, c]
            if add_fn is not None:
                s = s + add_fn(r0, j)
                if not keep_in_regs:
                    s_ref[r, c] = s
            if keep_in_regs:
                vals.append(s)
            part = s if part is None else jnp.maximum(part, s)
        m_new = jnp.maximum(m_prev, jnp.max(part, axis=-1, keepdims=True))
        for j in range(n_slabs):
            c = slice(col0 + j * HEAD_W, col0 + (j + 1) * HEAD_W)
            s = vals[j] if keep_in_regs else s_ref[r, c]
            p_ref[r, c] = jnp.exp(s - m_new).astype(BF16)
        alpha_ref[r, :] = jnp.exp(m_prev - m_new)
        m_ref[r, :] = m_new


def _accumulate(acc_ref, alpha_ref, p, v1):
    a = alpha_ref[...]
    acc_ref[...] = jnp.concatenate([a, a], axis=1) * acc_ref[...] + _dot(p, v1)


def _diff_prompt_kernel(q_ref, k_ref, v_ref, bkt_ref, rb_ref, lamp_ref, gn_ref, o_ref,
                        kbf, v1, qs, s_a, s_b, p_a, p_b, m_ref, alpha_a, alpha_b, acc_ref, bias_ref,
                        *, t, lam_init):
    h = pl.program_id(0)
    b = pl.program_id(1)
    qi = pl.program_id(2)

    @pl.when(qi == 0)
    def _():
        kbf[...] = k_ref[...].astype(BF16)
        v1[:, 0:HEAD_W] = v_ref[...].astype(BF16)
        v1[:, HEAD_W:2 * HEAD_W] = jnp.ones((v1.shape[0], HEAD_W), BF16)

    @pl.when(jnp.logical_and(b == 0, qi == 0))
    def _():
        ri = lax.broadcasted_iota(jnp.int32, (t, t), 0)
        ci = lax.broadcasted_iota(jnp.int32, (t, t), 1)
        bias_ref[0] = jnp.where(ci <= ri, 0.0, NEG_BIG)
        bias_ref[1] = jnp.zeros((t, t), F32)
        bias_ref[2] = jnp.zeros((t, t), F32)
        bri = lax.broadcasted_iota(jnp.int32, (BIAS_BAND, BIAS_BAND), 0)
        bci = lax.broadcasted_iota(jnp.int32, (BIAS_BAND, BIAS_BAND), 1)
        for rk in range(t // BIAS_BAND):
            r = slice(rk * BIAS_BAND, (rk + 1) * BIAS_BAND)
            bias_ref[0, r, r] = jnp.where(bci <= bri, _bias_from_buckets(bkt_ref[0, r, r], rb_ref, h), NEG_BIG)
            if rk > 0:
                c = slice((rk - 1) * BIAS_BAND, rk * BIAS_BAND)
                bias_ref[0, r, c] = _bias_from_buckets(bkt_ref[0, r, c], rb_ref, h)
        r, c = slice(0, BIAS_BAND), slice(t - BIAS_BAND, t)
        bias_ref[1, r, c] = _bias_from_buckets(bkt_ref[1, r, c], rb_ref, h)

    q = q_ref[...] * (DIFF_D ** -0.5)
    lane = lax.broadcasted_iota(jnp.int32, q.shape, 1)
    qs[0:t, :] = jnp.where(lane < DIFF_D, q, 0.0).astype(BF16)
    qs[t:2 * t, :] = jnp.where(lane >= DIFF_D, q, 0.0).astype(BF16)
    m_ref[...] = jnp.full(m_ref.shape, -jnp.inf, F32)
    acc_ref[...] = jnp.zeros(acc_ref.shape, F32)

    def scores(k, s_buf):
        rows = pl.ds(pl.multiple_of(jnp.minimum(k, qi) * t, t), t)
        s_buf[...] = _dot_nt(qs[...], kbf[rows, :])

    def update(k, s_buf, p_buf, alpha_buf):
        which = jnp.where(k == qi, 0, jnp.where(k == qi - 1, 1, 2))

        def add_fn(r0, j):
            q0 = r0 % t
            return bias_ref[which, q0:q0 + PROMPT_RB, j * HEAD_W:(j + 1) * HEAD_W]

        _softmax_tile(s_buf, p_buf, m_ref, alpha_buf, add_fn, rb=PROMPT_RB, keep_in_regs=True)
        rows = pl.ds(pl.multiple_of(k * t, t), t)
        _accumulate(acc_ref, alpha_buf, p_buf[...], v1[rows, :])

    n_tiles = qi + 1
    scores(0, s_a)

    def pair(j, carry):
        k0 = 2 * j
        scores(k0 + 1, s_b)
        update(k0, s_a, p_a, alpha_a)
        scores(k0 + 2, s_a)
        update(k0 + 1, s_b, p_b, alpha_b)
        return carry

    lax.fori_loop(0, n_tiles // 2, pair, 0)

    @pl.when(n_tiles % 2 == 1)
    def _():
        update(qi, s_a, p_a, alpha_a)

    lam = _lam(lamp_ref, lam_init)
    o = (acc_ref[0:t, 0:HEAD_W] / acc_ref[0:t, HEAD_W:2 * HEAD_W]
         - lam * (acc_ref[t:2 * t, 0:HEAD_W] / acc_ref[t:2 * t, HEAD_W:2 * HEAD_W]))
    o_ref[...] = (_rms(o, gn_ref[...]) * (1.0 - lam_init)).astype(o_ref.dtype)


def _diff_prompt(proj_a, dk, dv, rel_bias, lamp, gn, *, batch, seq, t, lam_init, name):
    nq = seq // t
    assert t % BIAS_BAND == 0 and BIAS_BAND >= MAX_DISTANCE
    pos = np.arange(t)
    bkt = jnp.asarray(np.stack([_t5_bucket(pos, pos), _t5_bucket(pos + t, pos)]))
    q_sec = (N_SECTIONS_A - 1) * N_HEADS
    return pl.pallas_call(
        functools.partial(_diff_prompt_kernel, t=t, lam_init=lam_init),
        grid=(N_HEADS, batch, nq),
        in_specs=[pl.BlockSpec((t, HEAD_W), lambda h, b, i: (b * nq + i, q_sec + h)),
                  pl.BlockSpec((seq, HEAD_W), lambda h, b, i: (b, h)),
                  pl.BlockSpec((seq, HEAD_W), lambda h, b, i: (b, h)),
                  pl.BlockSpec((2, t, t), lambda h, b, i: (0, 0, 0)),
                  pl.BlockSpec(memory_space=pltpu.SMEM),
                  pl.BlockSpec((4, DIFF_D), lambda h, b, i: (0, 0)),
                  pl.BlockSpec((1, HEAD_W), lambda h, b, i: (0, h))],
        out_specs=pl.BlockSpec((t, HEAD_W), lambda h, b, i: (b * nq + i, h)),
        out_shape=jax.ShapeDtypeStruct((batch * seq, GROUP_W), BF16),
        scratch_shapes=[pltpu.VMEM((seq, HEAD_W), BF16), pltpu.VMEM((seq, 2 * HEAD_W), BF16),
                        pltpu.VMEM((2 * t, HEAD_W), BF16),
                        pltpu.VMEM((2 * t, t), F32), pltpu.VMEM((2 * t, t), F32),
                        pltpu.VMEM((2 * t, t), BF16), pltpu.VMEM((2 * t, t), BF16),
                        pltpu.VMEM((2 * t, HEAD_W), F32), pltpu.VMEM((2 * t, HEAD_W), F32),
                        pltpu.VMEM((2 * t, HEAD_W), F32), pltpu.VMEM((2 * t, 2 * HEAD_W), F32),
                        pltpu.VMEM((3, t, t), F32)],
        compiler_params=_cparams(("arbitrary", "arbitrary", "arbitrary")),
        name=name,
    )(proj_a, dk, dv, bkt, rel_bias, lamp, gn)


def _diff_sample_kernel(pt_ref, q_ref, *refs, t_new, lam_init, n_pg, n_steps, n_batch):
    (ck_hbm, cv_hbm, kn_ref, vn_ref, bkt_last_ref, bkt_new_ref, rb_ref, lamp_ref, gn_ref, o_ref,
     qf, qbf, v1all, knew, v1new, s_ref, p_ref, sn_ref, pn_ref, m_ref, alpha_ref, acc_ref,
     mask_ref, bias_last_ref, bias_new_ref, kbuf, vbuf, sem) = refs
    b = pl.program_id(0)
    p = pl.program_id(1)
    last = n_steps - 1
    rows_per_head = 2 * t_new
    n_new = t_new * N_HEADS
    pg_rows = mask_ref.shape[1]

    def page_copies(bb, pp, slot):
        out = []
        for g in range(n_pg):
            page = pt_ref[bb, pp * n_pg + g]
            out.append(pltpu.make_async_copy(ck_hbm.at[page], kbuf.at[slot, g], sem.at[0, slot]))
            out.append(pltpu.make_async_copy(cv_hbm.at[page], vbuf.at[slot, g], sem.at[1, slot]))
        return out

    step = b * n_steps + p
    ahead = PAGE_SLOTS - 1

    @pl.when(step == 0)
    def _():
        for i in range(ahead):
            for c in page_copies(0, i, i):
                c.start()

    @pl.when(step + ahead < n_batch * n_steps)
    def _():
        pp = p + ahead
        wrap = (pp >= n_steps).astype(jnp.int32)
        for c in page_copies(b + wrap, pp - wrap * n_steps, lax.rem(step + ahead, PAGE_SLOTS)):
            c.start()

    slot = lax.rem(step, PAGE_SLOTS)
    for c in page_copies(b, p, slot):
        c.wait()

    def per_head_bias(bkt_ref):
        parts = []
        for h in range(N_HEADS):
            r0 = h * rows_per_head
            bk = bkt_ref[r0:r0 + rows_per_head, :]
            parts.append(jnp.where(bk >= 0, _bias_from_buckets(bk, rb_ref, h), NEG_BIG))
        return jnp.concatenate(parts, axis=0)

    @pl.when(jnp.logical_and(b == 0, p == 0))
    def _():
        mask_ref[...] = jnp.where(bkt_last_ref[...] >= 0, 0.0, NEG_BIG)
        bias_last_ref[...] = per_head_bias(bkt_last_ref)
        bias_new_ref[...] = per_head_bias(bkt_new_ref)
        knew[...] = jnp.zeros(knew.shape, BF16)
        v1new[:, 0:HEAD_W] = jnp.zeros((v1new.shape[0], HEAD_W), BF16)
        v1new[:, HEAD_W:2 * HEAD_W] = jnp.ones((v1new.shape[0], HEAD_W), BF16)
        v1all[:, HEAD_W:2 * HEAD_W] = jnp.ones((v1all.shape[0], HEAD_W), BF16)

    @pl.when(p == 0)
    def _():
        q = q_ref[...] * (DIFF_D ** -0.5)
        lane = lax.broadcasted_iota(jnp.int32, (t_new, HEAD_W), 1)
        for h in range(N_HEADS):
            cols = slice(h * HEAD_W, (h + 1) * HEAD_W)
            r0 = h * rows_per_head
            qf[r0:r0 + t_new, :] = jnp.where(lane < DIFF_D, q[:, cols], 0.0)
            qf[r0 + t_new:r0 + 2 * t_new, :] = jnp.where(lane >= DIFF_D, q[:, cols], 0.0)
        qbf[...] = qf[...].astype(BF16)
        m_ref[...] = jnp.full(m_ref.shape, -jnp.inf, F32)
        acc_ref[...] = jnp.zeros(acc_ref.shape, F32)

    def add_mask(r0, j):
        return mask_ref[r0:r0 + SAMPLE_RB, j * HEAD_W:(j + 1) * HEAD_W]

    def add_last(r0, j):
        return bias_last_ref[r0:r0 + SAMPLE_RB, j * HEAD_W:(j + 1) * HEAD_W]

    def page_update(g, add_fn):
        pg = slice(g * pg_rows, (g + 1) * pg_rows)
        _softmax_tile(s_ref, p_ref, m_ref, alpha_ref.at[g], add_fn, rb=SAMPLE_RB, keep_in_regs=False,
                      cols=(g * pg_rows, pg_rows))
        _accumulate(acc_ref, alpha_ref.at[g], p_ref[:, pg], v1all[pg, :])

    def page_scores(g):
        pg = slice(g * pg_rows, (g + 1) * pg_rows)
        v1all[pg, 0:HEAD_W] = vbuf[slot, g].astype(BF16)
        s_ref[:, pg] = _dot_nt(qbf[...], kbuf[slot, g].astype(BF16))

    page_scores(0)
    for g in range(n_pg - 1):
        page_scores(g + 1)
        page_update(g, add_mask)

    @pl.when(p < last)
    def _():
        page_update(n_pg - 1, add_mask)

    @pl.when(p == last)
    def _():
        page_update(n_pg - 1, add_last)
        knew[0:n_new, :] = kn_ref[0].astype(BF16)
        v1new[0:n_new, 0:HEAD_W] = vn_ref[0].astype(BF16)
        sn_ref[...] = _dot_nt(qbf[...], knew[...])
        _softmax_tile(sn_ref, pn_ref, m_ref, alpha_ref.at[0],
                      lambda r0, j: bias_new_ref[r0:r0 + SAMPLE_RB, :], rb=SAMPLE_RB, keep_in_regs=True)
        _accumulate(acc_ref, alpha_ref.at[0], pn_ref[...], v1new[...])

        lam = _lam(lamp_ref, lam_init)
        for h in range(N_HEADS):
            cols = slice(h * HEAD_W, (h + 1) * HEAD_W)
            r0 = h * rows_per_head
            r1 = r0 + t_new
            o = (acc_ref[r0:r1, 0:HEAD_W] / acc_ref[r0:r1, HEAD_W:2 * HEAD_W]
                 - lam * (acc_ref[r1:r1 + t_new, 0:HEAD_W] / acc_ref[r1:r1 + t_new, HEAD_W:2 * HEAD_W]))
            o_ref[:, cols] = (_rms(o, gn_ref[:, cols]) * (1.0 - lam_init)).astype(o_ref.dtype)


def _diff_sample(proj_a, dk, dv, cache_k, cache_v, page_table, rel_bias, lamp, gn, *, t_new, lam_init, name):
    batch, n_pages = page_table.shape
    n_phys, page = cache_k.shape[0], cache_k.shape[1]
    past = n_pages * page
    rows = N_HEADS * 2 * t_new
    kv_rows = page * N_HEADS
    new_rows = 128
    assert t_new * N_HEADS <= new_rows
    q_head = np.repeat(np.arange(N_HEADS), 2 * t_new)
    q_pos = past + np.tile(np.arange(t_new), 2 * N_HEADS)

    def buckets(n_cols, first_key_pos, n_valid_cols):
        col = np.arange(n_cols)
        k_pos = first_key_pos + col // N_HEADS
        ok = ((col % N_HEADS)[None, :] == q_head[:, None]) & (k_pos[None, :] <= q_pos[:, None])
        ok = ok & (col < n_valid_cols)[None, :]
        return jnp.asarray(np.where(ok, _t5_bucket(q_pos, k_pos), -1).astype(np.int32))

    bkt_last = buckets(kv_rows, past - page, kv_rows)
    bkt_new = buckets(new_rows, past, t_new * N_HEADS)
    n_pg = math.gcd(n_pages, PAGES_PER_STEP)
    n_steps = n_pages // n_pg
    assert PAGE_SLOTS - 1 <= n_steps, "the first batch element must cover the ring's head start"
    grid_spec = pltpu.PrefetchScalarGridSpec(
        num_scalar_prefetch=1,
        grid=(batch, n_steps),
        in_specs=[pl.BlockSpec((t_new, GROUP_W), lambda b, p, pt: (b, N_SECTIONS_A - 1)),
                  pl.BlockSpec(memory_space=pl.ANY),
                  pl.BlockSpec(memory_space=pl.ANY),
                  pl.BlockSpec((1, t_new * N_HEADS, HEAD_W), lambda b, p, pt: (b, 0, 0)),
                  pl.BlockSpec((1, t_new * N_HEADS, HEAD_W), lambda b, p, pt: (b, 0, 0)),
                  pl.BlockSpec((rows, kv_rows), lambda b, p, pt: (0, 0)),
                  pl.BlockSpec((rows, new_rows), lambda b, p, pt: (0, 0)),
                  pl.BlockSpec(memory_space=pltpu.SMEM),
                  pl.BlockSpec((4, DIFF_D), lambda b, p, pt: (0, 0)),
                  pl.BlockSpec((1, GROUP_W), lambda b, p, pt: (0, 0))],
        out_specs=pl.BlockSpec((t_new, GROUP_W), lambda b, p, pt: (b, 0)),
        scratch_shapes=[pltpu.VMEM((rows, HEAD_W), F32), pltpu.VMEM((rows, HEAD_W), BF16),
                        pltpu.VMEM((n_pg * kv_rows, 2 * HEAD_W), BF16),
                        pltpu.VMEM((new_rows, HEAD_W), BF16), pltpu.VMEM((new_rows, 2 * HEAD_W), BF16),
                        pltpu.VMEM((rows, n_pg * kv_rows), F32), pltpu.VMEM((rows, n_pg * kv_rows), BF16),
                        pltpu.VMEM((rows, new_rows), F32), pltpu.VMEM((rows, new_rows), BF16),
                        pltpu.VMEM((rows, HEAD_W), F32), pltpu.VMEM((n_pg, rows, HEAD_W), F32),
                        pltpu.VMEM((rows, 2 * HEAD_W), F32),
                        pltpu.VMEM((rows, kv_rows), F32), pltpu.VMEM((rows, kv_rows), F32),
                        pltpu.VMEM((rows, new_rows), F32),
                        pltpu.VMEM((PAGE_SLOTS, n_pg, kv_rows, HEAD_W), F32),
                        pltpu.VMEM((PAGE_SLOTS, n_pg, kv_rows, HEAD_W), F32),
                        pltpu.SemaphoreType.DMA((2, PAGE_SLOTS))],
    )
    return pl.pallas_call(
        functools.partial(_diff_sample_kernel, t_new=t_new, lam_init=lam_init, n_pg=n_pg, n_steps=n_steps,
                          n_batch=batch),
        grid_spec=grid_spec,
        out_shape=jax.ShapeDtypeStruct((batch * t_new, GROUP_W), F32),
        compiler_params=_cparams(("arbitrary", "arbitrary")),
        name=name,
    )(page_table, proj_a, cache_k.reshape(n_phys, kv_rows, HEAD_W), cache_v.reshape(n_phys, kv_rows, HEAD_W),
      dk.reshape(batch, t_new * N_HEADS, HEAD_W), dv.reshape(batch, t_new * N_HEADS, HEAD_W),
      bkt_last, bkt_new, rel_bias, lamp, gn)


def _mix_out_kernel(ret_ref, diff_ref, w_ret_ref, w_diff_ref, gain_ref, x_ref, gain_q_ref, wq_ref, *refs, has_mem):
    if has_mem:
        mk_ref, mv_ref, x1_ref, o_ref = refs
    else:
        x1_ref, o_ref = refs
    y = _dot(ret_ref[...].astype(BF16), w_ret_ref[...]) + _dot(diff_ref[...].astype(BF16), w_diff_ref[...])
    x1 = x_ref[...] + _rms(y, gain_ref[...])
    x1_ref[...] = x1
    q = _dot(_rms(x1, gain_q_ref[...]).astype(BF16), wq_ref[...])
    if not has_mem:
        o_ref[...] = q.astype(o_ref.dtype)
        return
    for h in range(N_XHEADS):
        cols = slice(h * HEAD_W, (h + 1) * HEAD_W)
        s = _dot_nt(q[:, cols].astype(BF16), mk_ref[0, :, cols].astype(BF16)) * (HEAD_W ** -0.5)
        e = jnp.exp(s - jnp.max(s, axis=-1, keepdims=True))
        p = e / jnp.sum(e, axis=-1, keepdims=True)
        o_ref[:, cols] = _dot(p.astype(BF16), mv_ref[0, :, cols].astype(BF16)).astype(o_ref.dtype)


def _mix_out(ret_o, diff_o, w_out, gain, x, gain_q, w_xq, mem, *, tm, rows_per_batch, out_dtype, name):
    m, n = x.shape
    ka = ret_o.shape[1]
    xw = w_xq.shape[1]
    in_specs = [pl.BlockSpec((tm, ka), lambda i: (i, 0)), pl.BlockSpec((tm, ka), lambda i: (i, 0)),
                pl.BlockSpec((ka, n), lambda i: (0, 0)), pl.BlockSpec((ka, n), lambda i: (1, 0)),
                pl.BlockSpec((1, n), lambda i: (0, 0)), pl.BlockSpec((tm, n), lambda i: (i, 0)),
                pl.BlockSpec((1, n), lambda i: (0, 0)), pl.BlockSpec((n, xw), lambda i: (0, 0))]
    args = [ret_o, diff_o, w_out, w_out, gain, x, gain_q, w_xq]
    if mem is not None:
        assert rows_per_batch % tm == 0
        per = rows_per_batch // tm
        n_mem = mem[0].shape[1]
        in_specs += [pl.BlockSpec((1, n_mem, xw), lambda i: (i // per, 0, 0))] * 2
        args += list(mem)
    return pl.pallas_call(
        functools.partial(_mix_out_kernel, has_mem=mem is not None),
        grid=(m // tm,),
        in_specs=in_specs,
        out_specs=[pl.BlockSpec((tm, n), lambda i: (i, 0)), pl.BlockSpec((tm, xw), lambda i: (i, 0))],
        out_shape=[jax.ShapeDtypeStruct((m, n), F32), jax.ShapeDtypeStruct((m, xw), out_dtype)],
        compiler_params=_cparams(("parallel",)),
        name=name,
    )(*args)


def _cross_interleaved_kernel(q_ref, mk_ref, mv_ref, o_ref):
    tq = q_ref.shape[1]
    q = q_ref[0]
    qs = jnp.concatenate([q[:, h * HEAD_W:(h + 1) * HEAD_W] for h in range(N_XHEADS)], axis=0).astype(BF16)
    s = _dot_nt(qs, mk_ref[0].astype(BF16)) * (HEAD_W ** -0.5)
    row_head = lax.broadcasted_iota(jnp.int32, s.shape, 0) // tq
    col_head = lax.broadcasted_iota(jnp.int32, s.shape, 1) % N_XHEADS
    s = jnp.where(row_head == col_head, s, NEG_BIG)
    e = jnp.exp(s - jnp.max(s, axis=-1, keepdims=True))
    p = e / jnp.sum(e, axis=-1, keepdims=True)
    o = _dot(p.astype(BF16), mv_ref[0].astype(BF16))
    for h in range(N_XHEADS):
        o_ref[0, :, h * HEAD_W:(h + 1) * HEAD_W] = o[h * tq:(h + 1) * tq, :].astype(o_ref.dtype)


def _cross_attend_interleaved(q, mk, mv, *, name):
    b, t, w = q.shape
    rows = mk.shape[1]
    return pl.pallas_call(
        _cross_interleaved_kernel,
        grid=(b,),
        in_specs=[pl.BlockSpec((1, t, w), lambda i: (i, 0, 0)),
                  pl.BlockSpec((1, rows, HEAD_W), lambda i: (i, 0, 0)),
                  pl.BlockSpec((1, rows, HEAD_W), lambda i: (i, 0, 0))],
        out_specs=pl.BlockSpec((1, t, w), lambda i: (i, 0, 0)),
        out_shape=jax.ShapeDtypeStruct((b, t, w), q.dtype),
        compiler_params=_cparams(("parallel",)),
        name=name,
    )(q, mk, mv)


def _xo_ffn_kernel(x_ref, o_ref, wxo_ref, gx_ref, gpre_ref, wg_ref, wu_ref, wd_ref, gpost_ref, y_ref, *rest):
    x2_ref, h_ref, acc_ref = rest[-3:]
    copies = rest[:-3]
    f = pl.program_id(1)

    def weight(w_ref, i):
        w = w_ref[...]
        if copies:
            w = w.astype(BF16)
            copies[i][...] = w
        return w

    @pl.when(f == 0)
    def _():
        x2 = x_ref[...] + _rms(_dot(o_ref[...].astype(BF16), weight(wxo_ref, 0)), gx_ref[...])
        x2_ref[...] = x2
        h_ref[...] = _rms(x2, gpre_ref[...]).astype(BF16)
        acc_ref[...] = jnp.zeros(acc_ref.shape, F32)

    h = h_ref[...]
    g = _dot(h, weight(wg_ref, 1))
    a = (g * jax.nn.sigmoid(g)) * _dot(h, weight(wu_ref, 2))
    acc_ref[...] += _dot(a.astype(BF16), weight(wd_ref, 3))

    @pl.when(f == pl.num_programs(1) - 1)
    def _():
        y_ref[...] = x2_ref[...] + _rms(acc_ref[...], gpost_ref[...])


def _xo_ffn(x, o, w_xo, gx, gpre, wg, wu, wd, gpost, *, tm, tf, name):
    m, d = x.shape
    xw = o.shape[1]
    dff = wg.shape[1]
    emit_w = wg.dtype != BF16
    assert not emit_w or m == tm, "each weight tile must be visited once to be copied out"
    w_specs = [pl.BlockSpec((xw, d), lambda i, f: (0, 0)),
               pl.BlockSpec((d, tf), lambda i, f: (0, f)),
               pl.BlockSpec((d, tf), lambda i, f: (0, f)),
               pl.BlockSpec((tf, d), lambda i, f: (f, 0))]
    out_specs = [pl.BlockSpec((tm, d), lambda i, f: (i, 0))]
    out_shape = [jax.ShapeDtypeStruct((m, d), F32)]
    if emit_w:
        out_specs += w_specs
        out_shape += [jax.ShapeDtypeStruct(a.shape, BF16) for a in (w_xo, wg, wu, wd)]
    out = pl.pallas_call(
        _xo_ffn_kernel,
        grid=(m // tm, dff // tf),
        in_specs=[pl.BlockSpec((tm, d), lambda i, f: (i, 0)),
                  pl.BlockSpec((tm, xw), lambda i, f: (i, 0)),
                  w_specs[0],
                  pl.BlockSpec((1, d), lambda i, f: (0, 0)),
                  pl.BlockSpec((1, d), lambda i, f: (0, 0)),
                  w_specs[1], w_specs[2], w_specs[3],
                  pl.BlockSpec((1, d), lambda i, f: (0, 0))],
        out_specs=out_specs,
        out_shape=out_shape,
        scratch_shapes=[pltpu.VMEM((tm, d), F32), pltpu.VMEM((tm, d), BF16), pltpu.VMEM((tm, d), F32)],
        compiler_params=_cparams(("parallel", "arbitrary")),
        name=name,
    )(x, o, w_xo, gx, gpre, wg, wu, wd, gpost)
    return out if emit_w else out[0]


def _rope_tables(pos):
    inv_freq = ROPE_BASE ** (-jnp.arange(0, HEAD_W, 2, dtype=F32) / HEAD_W)
    ang = pos.astype(F32)[:, None] * inv_freq[None, :]
    cos = jnp.repeat(jnp.cos(ang), 2, axis=-1)
    sin = jnp.sin(ang)
    return cos, jnp.stack([-sin, sin], axis=-1).reshape(cos.shape)


def _layer(x, pos, s0, mem_k, mem_v, paged, w, *, batch, seq, lam_init, tag):
    m = batch * seq
    big = paged is None
    tm = 512 if big else m
    w = dict(w)
    proj_a, dk, dv, *w_in_copy = _in_proj(x, w["n_pre_mix"], w["w_in"], tm=1024 if big else m, tn=512,
                                          name=f"in_proj_{tag}")
    if w_in_copy:
        w["w_in"] = w_in_copy[0]
    cos, sin = _rope_tables(pos)
    ret_o, ret_s = _retention(proj_a, cos, sin, w["lg"], w["ret_gn"], s0, batch=batch, seq=seq,
                              heads=1 if big else N_HEADS, out_dtype=BF16 if big else F32, name=f"retention_{tag}")
    if big:
        diff_o = _diff_prompt(proj_a, dk, dv, w["rel_bias"], w["lamp"], w["diff_gn"], batch=batch, seq=seq,
                              t=512, lam_init=lam_init, name=f"diff_attn_{tag}")
    else:
        diff_o = _diff_sample(proj_a, dk, dv, *paged, w["rel_bias"], w["lamp"], w["diff_gn"], t_new=seq,
                              lam_init=lam_init, name=f"diff_attn_{tag}")
    xw = N_XHEADS * HEAD_W
    x, o = _mix_out(ret_o, diff_o, w["w_out"], w["n_post_mix"], x, w["n_pre_x"], w["w_xq"],
                    (mem_k, mem_v) if big else None, tm=tm, rows_per_batch=seq,
                    out_dtype=BF16 if big else F32, name=f"mix_out_{tag}")
    if not big:
        o = _cross_attend_interleaved(o.reshape(batch, seq, xw), mem_k, mem_v, name=f"cross_{tag}").reshape(m, xw)
    out = _xo_ffn(x, o, w["w_xo"], w["n_post_x"], w["n_pre_ffn"], w["w_gate"], w["w_up"], w["w_down"],
                  w["n_post_ffn"], tm=tm, tf=512 if big else 256, name=f"ffn_{tag}")
    if isinstance(out, (list, tuple)):
        x, w["w_xo"], w["w_gate"], w["w_up"], w["w_down"] = out
    else:
        x = out
    return x, dk, dv, ret_s, w


def kernel(x_prompt, x_sample, cache_k, cache_v, state_ret, cache_mem_k, cache_mem_v, page_table, mem_prompt,
           rel_bias, norm_pre_mix, norm_post_mix, norm_pre_x, norm_post_x, norm_pre_ffn, norm_post_ffn, norm_mem,
           w_in, w_out, ret_gn, diff_gn, lam_q1, lam_k1, lam_q2, lam_k2,
           w_xq, w_xk, w_xv, w_xo, w_gate, w_up, w_down):
    b_p, t_p, d = x_prompt.shape
    b_s, t_s, _ = x_sample.shape
    depth = w_in.shape[0]
    n_mem = mem_prompt.shape[1]
    xw = N_XHEADS * HEAD_W
    past_len = page_table.shape[1] * cache_k.shape[2]
    pos_p = jnp.arange(t_p)
    pos_s = past_len + jnp.arange(t_s)
    log_g = jnp.log1p(-(2.0 ** (-5.0 - jnp.arange(N_HEADS, dtype=F32))))
    lg = jnp.broadcast_to(log_g[:, None, None], (N_HEADS, 8, HEAD_W))
    xp = x_prompt.reshape(b_p * t_p, d)
    xs = x_sample.reshape(b_s * t_s, d)
    outs = [[] for _ in range(8)]
    for l in range(depth):
        lam_init = 0.8 - 0.6 * math.exp(-0.3 * l)
        w = {
            "n_pre_mix": norm_pre_mix[l][None], "n_post_mix": norm_post_mix[l][None],
            "n_pre_x": norm_pre_x[l][None], "n_post_x": norm_post_x[l][None],
            "n_pre_ffn": norm_pre_ffn[l][None], "n_post_ffn": norm_post_ffn[l][None],
            "w_in": w_in[l], "w_out": w_out[l].astype(BF16),
            "ret_gn": ret_gn[l][None], "diff_gn": diff_gn[l][None],
            "lamp": jnp.stack([lam_q1[l], lam_k1[l], lam_q2[l], lam_k2[l]]),
            "w_xq": w_xq[l].astype(BF16), "w_xo": w_xo[l],
            "w_gate": w_gate[l], "w_up": w_up[l], "w_down": w_down[l],
            "rel_bias": rel_bias, "lg": lg,
        }
        mem_flat = mem_prompt.reshape(b_p * n_mem, d)
        mk_p, mv_p = _norm_matmul_pair(mem_flat, norm_mem[l][None], w_xk[l], w_xv[l], name="mem_kv")
        s0 = jnp.zeros((b_p, N_HEADS, HEAD_W, HEAD_W), F32)
        xs, dk_s, dv_s, s_s, w = _layer(xs, pos_s, state_ret[l],
                                        cache_mem_k[l].reshape(b_s, n_mem * N_XHEADS, HEAD_W),
                                        cache_mem_v[l].reshape(b_s, n_mem * N_XHEADS, HEAD_W),
                                        (cache_k[l], cache_v[l], page_table), w,
                                        batch=b_s, seq=t_s, lam_init=lam_init, tag="sample")
        xp, dk_p, dv_p, s_p, _ = _layer(xp, pos_p, s0, mk_p.reshape(b_p, n_mem, xw), mv_p.reshape(b_p, n_mem, xw),
                                        None, w, batch=b_p, seq=t_p, lam_init=lam_init, tag="prompt")
        per_layer = (dk_p.reshape(b_p, t_p, N_HEADS, HEAD_W), dv_p.reshape(b_p, t_p, N_HEADS, HEAD_W), s_p,
                     mk_p.reshape(b_p, n_mem, N_XHEADS, HEAD_W), mv_p.reshape(b_p, n_mem, N_XHEADS, HEAD_W),
                     dk_s.reshape(b_s, t_s, N_HEADS, HEAD_W), dv_s.reshape(b_s, t_s, N_HEADS, HEAD_W), s_s)
        for acc, val in zip(outs, per_layer):
            acc.append(val)
    stacked = [jnp.stack(o) for o in outs]
    return (xp.reshape(b_p, t_p, d), xs.reshape(b_s, t_s, d), *stacked)
```

```python
import functools
import math

import jax
import jax.numpy as jnp
import numpy as np
from jax import lax
from jax.experimental import pallas as pl
from jax.experimental.pallas import tpu as pltpu

F32 = jnp.float32
BF16 = jnp.bfloat16

D_MODEL = 2048
N_HEADS = 8
HEAD_W = 128
GROUP_W = N_HEADS * HEAD_W
DIFF_D = 64
ROPE_BASE = 10000.0
RET_CHUNK = 128
N_BUCKETS = 32
MAX_EXACT = N_BUCKETS // 2
MAX_DISTANCE = 128
BIAS_BAND = 128
N_XHEADS = 4
RMS_EPS = 1e-6
NEG_BIG = -1e30
LOG2E = math.log2(math.e)
PROMPT_RB = 32
SAMPLE_RB = 16
PAGES_PER_STEP = 8
PAGE_SLOTS = 4
RET_UNROLL = 8
N_SECTIONS_A = 5
VMEM_LIMIT = 56 * 1024 * 1024


def _cparams(sem):
    return pltpu.CompilerParams(dimension_semantics=sem, vmem_limit_bytes=VMEM_LIMIT)


def _rms(x, gain):
    return x * lax.rsqrt(jnp.mean(x * x, axis=-1, keepdims=True) + RMS_EPS) * gain


def _dot(a, b):
    return jnp.dot(a, b, preferred_element_type=F32)


def _dot_nt(a, b):
    return lax.dot_general(a, b, (((1,), (1,)), ((), ())), preferred_element_type=F32)


def _dot_tn(a, b):
    return lax.dot_general(a, b, (((0,), (0,)), ((), ())), preferred_element_type=F32)


def _norm_matmul_pair_kernel(x_ref, g_ref, wa_ref, wb_ref, oa_ref, ob_ref):
    h = _rms(x_ref[...], g_ref[...]).astype(BF16)
    oa_ref[...] = _dot(h, wa_ref[...].astype(BF16))
    ob_ref[...] = _dot(h, wb_ref[...].astype(BF16))


def _norm_matmul_pair(x, gain, wa, wb, *, name):
    m, k = x.shape
    n = wa.shape[1]
    full = lambda shape: pl.BlockSpec(shape, lambda i: (0, 0))
    return pl.pallas_call(
        _norm_matmul_pair_kernel,
        grid=(1,),
        in_specs=[full((m, k)), full((1, k)), full((k, n)), full((k, n))],
        out_specs=[full((m, n)), full((m, n))],
        out_shape=[jax.ShapeDtypeStruct((m, n), F32)] * 2,
        compiler_params=_cparams(("arbitrary",)),
        name=name,
    )(x, gain, wa, wb)


def _in_proj_kernel(x_ref, g_ref, w_ref, a_ref, dk_ref, dv_ref, *rest, n_a, n_g):
    j = pl.program_id(1)
    h_ref = rest[-1]

    @pl.when(j == 0)
    def _():
        h_ref[...] = _rms(x_ref[...], g_ref[...]).astype(BF16)

    w = w_ref[...]
    if len(rest) == 2:
        w = w.astype(BF16)
        rest[0][...] = w
    res = _dot(h_ref[...], w)

    @pl.when(j < n_a)
    def _():
        a_ref[...] = res

    @pl.when(jnp.logical_and(j >= n_a, j < n_a + n_g))
    def _():
        dk_ref[...] = res

    @pl.when(j >= n_a + n_g)
    def _():
        dv_ref[...] = res


def _in_proj(x, gain, w, *, tm, tn, name):
    m, k = x.shape
    n = w.shape[1]
    n_g = GROUP_W // tn
    n_a = N_SECTIONS_A * n_g
    emit_w = w.dtype != BF16
    assert not emit_w or m == tm, "each weight tile must be visited once to be copied out"
    out_specs = [pl.BlockSpec((tm, tn), lambda i, j: (i, jnp.minimum(j, n_a - 1))),
                 pl.BlockSpec((tm, tn), lambda i, j: (i, jnp.clip(j - n_a, 0, n_g - 1))),
                 pl.BlockSpec((tm, tn), lambda i, j: (i, jnp.clip(j - n_a - n_g, 0, n_g - 1)))]
    out_shape = [jax.ShapeDtypeStruct((m, N_SECTIONS_A * GROUP_W), F32),
                 jax.ShapeDtypeStruct((m, GROUP_W), F32),
                 jax.ShapeDtypeStruct((m, GROUP_W), F32)]
    if emit_w:
        out_specs.append(pl.BlockSpec((k, tn), lambda i, j: (0, j)))
        out_shape.append(jax.ShapeDtypeStruct((k, n), BF16))
    return pl.pallas_call(
        functools.partial(_in_proj_kernel, n_a=n_a, n_g=n_g),
        grid=(m // tm, n // tn),
        in_specs=[pl.BlockSpec((tm, k), lambda i, j: (i, 0)),
                  pl.BlockSpec((1, k), lambda i, j: (0, 0)),
                  pl.BlockSpec((k, tn), lambda i, j: (0, j))],
        out_specs=out_specs,
        out_shape=out_shape,
        scratch_shapes=[pltpu.VMEM((tm, k), BF16)],
        compiler_params=_cparams(("parallel", "arbitrary")),
        name=name,
    )(x, gain, w)


def _rotate(x, cos, sin_signed):
    lane = lax.broadcasted_iota(jnp.int32, x.shape, 1)
    nxt = pltpu.roll(x, HEAD_W - 1, 1)
    prv = pltpu.roll(x, 1, 1)
    return x * cos + jnp.where(lane % 2 == 0, nxt, prv) * sin_signed


def _retention_kernel(q_ref, k_ref, v_ref, g_ref, cos_ref, sin_ref, lg_ref, gn_ref, s0_ref,
                      o_ref, sf_ref, kr_ref, st_ref, *, chunk, n_chunks, heads):
    rows_mm = RET_CHUNK
    ri = lax.broadcasted_iota(jnp.int32, (rows_mm, rows_mm), 0)
    ci = lax.broadcasted_iota(jnp.int32, (rows_mm, rows_mm), 1)
    rel = (ri - ci).astype(F32)
    idx = lax.broadcasted_iota(jnp.int32, (rows_mm, 1), 0).astype(F32)

    def pad(a):
        if chunk == rows_mm:
            return a
        return jnp.concatenate([a, jnp.zeros((rows_mm - chunk, a.shape[1]), a.dtype)], axis=0)

    for hh in range(heads):
        cols = slice(hh * HEAD_W, (hh + 1) * HEAD_W)
        lg = lg_ref[hh, 0:1, 0:1]
        intra = jnp.where(rel >= 0, jnp.exp(lg * jnp.maximum(rel, 0.0)), 0.0)
        q_dec = jnp.exp(lg * (idx + 1.0))
        k_dec = jnp.exp(lg * jnp.maximum(chunk - 1.0 - idx, 0.0))
        chunk_dec = jnp.exp(lg * float(chunk))
        gn = gn_ref[:, cols]

        def chunk_rows(c):
            if n_chunks == 1:
                return slice(0, chunk), slice(0, rows_mm)
            return (pl.ds(pl.multiple_of(c * chunk, chunk), chunk),
                    pl.ds(pl.multiple_of(c * rows_mm, rows_mm), rows_mm))

        def state_step(c, s, hh=hh, cols=cols, k_dec=k_dec, chunk_dec=chunk_dec):
            rows, rows_p = chunk_rows(c)
            k = pad(_rotate(k_ref[rows, cols], cos_ref[rows, :], sin_ref[rows, :]) * (HEAD_W ** -0.5))
            kr_ref[rows_p, cols] = k.astype(BF16)
            st_ref[hh, c] = s.astype(BF16)
            vb = pad(v_ref[rows, cols]).astype(BF16)
            return s * chunk_dec + _dot_tn((k * k_dec).astype(BF16), vb)

        def output_step(c, carry, hh=hh, cols=cols, intra=intra, q_dec=q_dec, gn=gn):
            rows, rows_p = chunk_rows(c)
            qb = _rotate(q_ref[rows, cols], cos_ref[rows, :], sin_ref[rows, :]).astype(BF16)
            vb = pad(v_ref[rows, cols]).astype(BF16)
            scores = _dot_nt(qb, kr_ref[rows_p, cols]) * intra[0:chunk, :]
            o = _dot(scores.astype(BF16), vb) + _dot(qb, st_ref[hh, c]) * q_dec[0:chunk, :]
            gate = g_ref[rows, cols]
            y = _rms(o, gn)
            o_ref[rows, cols] = ((gate * jax.nn.sigmoid(gate)) * y).astype(o_ref.dtype)
            return carry

        s0 = s0_ref[0, hh]
        if n_chunks == 1:
            sf_ref[0, hh] = state_step(0, s0)
            output_step(0, 0)
        else:
            unroll = math.gcd(n_chunks, RET_UNROLL)
            sf_ref[0, hh] = lax.fori_loop(0, n_chunks, state_step, s0, unroll=unroll)
            lax.fori_loop(0, n_chunks, output_step, 0, unroll=unroll)


def _retention(proj_a, cos, sin, lg, gn, s0, *, batch, seq, heads, out_dtype, name):
    chunk = math.gcd(seq, RET_CHUNK)
    w = heads * HEAD_W
    n_hb = N_HEADS // heads

    def sec(s):
        return pl.BlockSpec((seq, w), lambda b, h, s=s: (b, s * n_hb + h))

    return pl.pallas_call(
        functools.partial(_retention_kernel, chunk=chunk, n_chunks=seq // chunk, heads=heads),
        grid=(batch, n_hb),
        in_specs=[sec(0), sec(1), sec(2), sec(3),
                  pl.BlockSpec((seq, HEAD_W), lambda b, h: (0, 0)),
                  pl.BlockSpec((seq, HEAD_W), lambda b, h: (0, 0)),
                  pl.BlockSpec((heads, 8, HEAD_W), lambda b, h: (h, 0, 0)),
                  pl.BlockSpec((1, w), lambda b, h: (0, h)),
                  pl.BlockSpec((1, heads, HEAD_W, HEAD_W), lambda b, h: (b, h, 0, 0))],
        out_specs=[pl.BlockSpec((seq, w), lambda b, h: (b, h)),
                   pl.BlockSpec((1, heads, HEAD_W, HEAD_W), lambda b, h: (b, h, 0, 0))],
        out_shape=[jax.ShapeDtypeStruct((batch * seq, GROUP_W), out_dtype),
                   jax.ShapeDtypeStruct((batch, N_HEADS, HEAD_W, HEAD_W), F32)],
        scratch_shapes=[pltpu.VMEM((seq // chunk * RET_CHUNK, w), BF16),
                        pltpu.VMEM((heads, seq // chunk, HEAD_W, HEAD_W), BF16)],
        compiler_params=_cparams(("parallel", "parallel")),
        name=name,
    )(proj_a, proj_a, proj_a, proj_a, cos, sin, lg, gn, s0)


def _t5_bucket(qpos, kpos):
    n = np.maximum(qpos[:, None] - kpos[None, :], 0)
    nf = np.maximum(n, 1).astype(np.float32)
    large = MAX_EXACT + (np.log(nf / np.float32(MAX_EXACT)) / np.float32(math.log(MAX_DISTANCE / MAX_EXACT))
                         * np.float32(N_BUCKETS - MAX_EXACT)).astype(np.int32)
    large = np.minimum(large, N_BUCKETS - 1)
    return np.where(n < MAX_EXACT, n, large).astype(np.int32)


def _bias_from_buckets(bkt, rb_ref, h):
    far = rb_ref[N_BUCKETS - 1, h]
    b = jnp.zeros(bkt.shape, F32)
    for u in range(N_BUCKETS - 1):
        b = jnp.where(bkt == u, (rb_ref[u, h] - far) * LOG2E, b)
    return b


def _lam(lamp_ref, lam_init):
    e1 = jnp.exp(jnp.sum(lamp_ref[0:1, :] * lamp_ref[1:2, :], axis=-1, keepdims=True))
    e2 = jnp.exp(jnp.sum(lamp_ref[2:3, :] * lamp_ref[3:4, :], axis=-1, keepdims=True))
    return e1 - e2 + lam_init


def _softmax_tile(s_ref, p_ref, m_ref, alpha_ref, add_fn, *, rb, keep_in_regs, cols=None):
    n_rows = s_ref.shape[0]
    col0, n_cols = (0, s_ref.shape[1]) if cols is None else cols
    n_slabs = n_cols // HEAD_W
    for r0 in range(0, n_rows, rb):
        r = slice(r0, r0 + rb)
        m_prev = m_ref[r, :]
        vals, part = [], None
        for j in range(n_slabs):
            c = slice(col0 + j * HEAD_W, col0 + (j + 1) * HEAD_W)
            s = s_ref[r, c]
            if add_fn is not None:
                s = s + add_fn(r0, j)
                if not keep_in_regs:
                    s_ref[r, c] = s
            if keep_in_regs:
                vals.append(s)
            part = s if part is None else jnp.maximum(part, s)
        m_new = jnp.maximum(m_prev, jnp.max(part, axis=-1, keepdims=True))
        for j in range(n_slabs):
            c = slice(col0 + j * HEAD_W, col0 + (j + 1) * HEAD_W)
            s = vals[j] if keep_in_regs else s_ref[r, c]
            p_ref[r, c] = jnp.exp2(s - m_new).astype(BF16)
        alpha_ref[r, :] = jnp.exp2(m_prev - m_new)
        m_ref[r, :] = m_new


def _accumulate(acc_ref, alpha_ref, p, v1):
    a = alpha_ref[...]
    acc_ref[...] = jnp.concatenate([a, a], axis=1) * acc_ref[...] + _dot(p, v1)


def _diff_prompt_kernel(q_ref, k_ref, v_ref, bkt_ref, rb_ref, lamp_ref, gn_ref, o_ref,
                        kbf, v1, qs, s_a, s_b, p_a, p_b, m_ref, alpha_a, alpha_b, acc_ref, bias_ref,
                        *, t, lam_init):
    h = pl.program_id(0)
    b = pl.program_id(1)
    qi = pl.program_id(2)

    @pl.when(qi == 0)
    def _():
        kbf[...] = k_ref[...].astype(BF16)
        v1[:, 0:HEAD_W] = v_ref[...].astype(BF16)
        v1[:, HEAD_W:2 * HEAD_W] = jnp.ones((v1.shape[0], HEAD_W), BF16)

    @pl.when(jnp.logical_and(b == 0, qi == 0))
    def _():
        ri = lax.broadcasted_iota(jnp.int32, (t, t), 0)
        ci = lax.broadcasted_iota(jnp.int32, (t, t), 1)
        bias_ref[0] = jnp.where(ci <= ri, 0.0, NEG_BIG)
        bias_ref[1] = jnp.zeros((t, t), F32)
        bias_ref[2] = jnp.zeros((t, t), F32)
        bri = lax.broadcasted_iota(jnp.int32, (BIAS_BAND, BIAS_BAND), 0)
        bci = lax.broadcasted_iota(jnp.int32, (BIAS_BAND, BIAS_BAND), 1)
        for rk in range(t // BIAS_BAND):
            r = slice(rk * BIAS_BAND, (rk + 1) * BIAS_BAND)
            bias_ref[0, r, r] = jnp.where(bci <= bri, _bias_from_buckets(bkt_ref[0, r, r], rb_ref, h), NEG_BIG)
            if rk > 0:
                c = slice((rk - 1) * BIAS_BAND, rk * BIAS_BAND)
                bias_ref[0, r, c] = _bias_from_buckets(bkt_ref[0, r, c], rb_ref, h)
        r, c = slice(0, BIAS_BAND), slice(t - BIAS_BAND, t)
        bias_ref[1, r, c] = _bias_from_buckets(bkt_ref[1, r, c], rb_ref, h)

    q = q_ref[...] * (DIFF_D ** -0.5 * LOG2E)
    lane = lax.broadcasted_iota(jnp.int32, q.shape, 1)
    qs[0:t, :] = jnp.where(lane < DIFF_D, q, 0.0).astype(BF16)
    qs[t:2 * t, :] = jnp.where(lane >= DIFF_D, q, 0.0).astype(BF16)
    m_ref[...] = jnp.full(m_ref.shape, -jnp.inf, F32)
    acc_ref[...] = jnp.zeros(acc_ref.shape, F32)

    def scores(k, s_buf):
        rows = pl.ds(pl.multiple_of(jnp.minimum(k, qi) * t, t), t)
        s_buf[...] = _dot_nt(qs[...], kbf[rows, :])

    def update(k, s_buf, p_buf, alpha_buf):
        which = jnp.where(k == qi, 0, jnp.where(k == qi - 1, 1, 2))

        def add_fn(r0, j):
            q0 = r0 % t
            return bias_ref[which, q0:q0 + PROMPT_RB, j * HEAD_W:(j + 1) * HEAD_W]

        _softmax_tile(s_buf, p_buf, m_ref, alpha_buf, add_fn, rb=PROMPT_RB, keep_in_regs=True)
        rows = pl.ds(pl.multiple_of(k * t, t), t)
        _accumulate(acc_ref, alpha_buf, p_buf[...], v1[rows, :])

    n_tiles = qi + 1
    scores(0, s_a)

    def pair(j, carry):
        k0 = 2 * j
        scores(k0 + 1, s_b)
        update(k0, s_a, p_a, alpha_a)
        scores(k0 + 2, s_a)
        update(k0 + 1, s_b, p_b, alpha_b)
        return carry

    lax.fori_loop(0, n_tiles // 2, pair, 0)

    @pl.when(n_tiles % 2 == 1)
    def _():
        update(qi, s_a, p_a, alpha_a)

    lam = _lam(lamp_ref, lam_init)
    o = (acc_ref[0:t, 0:HEAD_W] / acc_ref[0:t, HEAD_W:2 * HEAD_W]
         - lam * (acc_ref[t:2 * t, 0:HEAD_W] / acc_ref[t:2 * t, HEAD_W:2 * HEAD_W]))
    o_ref[...] = (_rms(o, gn_ref[...]) * (1.0 - lam_init)).astype(o_ref.dtype)


def _diff_prompt(proj_a, dk, dv, rel_bias, lamp, gn, *, batch, seq, t, lam_init, name):
    nq = seq // t
    assert t % BIAS_BAND == 0 and BIAS_BAND >= MAX_DISTANCE
    pos = np.arange(t)
    bkt = jnp.asarray(np.stack([_t5_bucket(pos, pos), _t5_bucket(pos + t, pos)]))
    q_sec = (N_SECTIONS_A - 1) * N_HEADS
    return pl.pallas_call(
        functools.partial(_diff_prompt_kernel, t=t, lam_init=lam_init),
        grid=(N_HEADS, batch, nq),
        in_specs=[pl.BlockSpec((t, HEAD_W), lambda h, b, i: (b * nq + i, q_sec + h)),
                  pl.BlockSpec((seq, HEAD_W), lambda h, b, i: (b, h)),
                  pl.BlockSpec((seq, HEAD_W), lambda h, b, i: (b, h)),
                  pl.BlockSpec((2, t, t), lambda h, b, i: (0, 0, 0)),
                  pl.BlockSpec(memory_space=pltpu.SMEM),
                  pl.BlockSpec((4, DIFF_D), lambda h, b, i: (0, 0)),
                  pl.BlockSpec((1, HEAD_W), lambda h, b, i: (0, h))],
        out_specs=pl.BlockSpec((t, HEAD_W), lambda h, b, i: (b * nq + i, h)),
        out_shape=jax.ShapeDtypeStruct((batch * seq, GROUP_W), BF16),
        scratch_shapes=[pltpu.VMEM((seq, HEAD_W), BF16), pltpu.VMEM((seq, 2 * HEAD_W), BF16),
                        pltpu.VMEM((2 * t, HEAD_W), BF16),
                        pltpu.VMEM((2 * t, t), F32), pltpu.VMEM((2 * t, t), F32),
                        pltpu.VMEM((2 * t, t), BF16), pltpu.VMEM((2 * t, t), BF16),
                        pltpu.VMEM((2 * t, HEAD_W), F32), pltpu.VMEM((2 * t, HEAD_W), F32),
                        pltpu.VMEM((2 * t, HEAD_W), F32), pltpu.VMEM((2 * t, 2 * HEAD_W), F32),
                        pltpu.VMEM((3, t, t), F32)],
        compiler_params=_cparams(("arbitrary", "arbitrary", "arbitrary")),
        name=name,
    )(proj_a, dk, dv, bkt, rel_bias, lamp, gn)


def _diff_sample_kernel(pt_ref, q_ref, *refs, t_new, lam_init, n_pg, n_steps, n_batch):
    (ck_hbm, cv_hbm, kn_ref, vn_ref, bkt_last_ref, bkt_new_ref, rb_ref, lamp_ref, gn_ref, o_ref,
     qf, qbf, v1all, knew, v1new, s_ref, p_ref, sn_ref, pn_ref, m_ref, alpha_ref, acc_ref,
     mask_ref, bias_last_ref, bias_new_ref, kbuf, vbuf, sem) = refs
    b = pl.program_id(0)
    p = pl.program_id(1)
    last = n_steps - 1
    rows_per_head = 2 * t_new
    n_new = t_new * N_HEADS
    pg_rows = mask_ref.shape[1]

    def page_copies(bb, pp, slot):
        out = []
        for g in range(n_pg):
            page = pt_ref[bb, pp * n_pg + g]
            out.append(pltpu.make_async_copy(ck_hbm.at[page], kbuf.at[slot, g], sem.at[0, slot]))
            out.append(pltpu.make_async_copy(cv_hbm.at[page], vbuf.at[slot, g], sem.at[1, slot]))
        return out

    step = b * n_steps + p
    ahead = PAGE_SLOTS - 1

    @pl.when(step == 0)
    def _():
        for i in range(ahead):
            for c in page_copies(0, i, i):
                c.start()

    @pl.when(step + ahead < n_batch * n_steps)
    def _():
        pp = p + ahead
        wrap = (pp >= n_steps).astype(jnp.int32)
        for c in page_copies(b + wrap, pp - wrap * n_steps, lax.rem(step + ahead, PAGE_SLOTS)):
            c.start()

    slot = lax.rem(step, PAGE_SLOTS)
    for c in page_copies(b, p, slot):
        c.wait()

    def per_head_bias(bkt_ref):
        parts = []
        for h in range(N_HEADS):
            r0 = h * rows_per_head
            bk = bkt_ref[r0:r0 + rows_per_head, :]
            parts.append(jnp.where(bk >= 0, _bias_from_buckets(bk, rb_ref, h), NEG_BIG))
        return jnp.concatenate(parts, axis=0)

    @pl.when(jnp.logical_and(b == 0, p == 0))
    def _():
        mask_ref[...] = jnp.where(bkt_last_ref[...] >= 0, 0.0, NEG_BIG)
        bias_last_ref[...] = per_head_bias(bkt_last_ref)
        bias_new_ref[...] = per_head_bias(bkt_new_ref)
        knew[...] = jnp.zeros(knew.shape, BF16)
        v1new[:, 0:HEAD_W] = jnp.zeros((v1new.shape[0], HEAD_W), BF16)
        v1new[:, HEAD_W:2 * HEAD_W] = jnp.ones((v1new.shape[0], HEAD_W), BF16)
        v1all[:, HEAD_W:2 * HEAD_W] = jnp.ones((v1all.shape[0], HEAD_W), BF16)

    @pl.when(p == 0)
    def _():
        q = q_ref[...] * (DIFF_D ** -0.5 * LOG2E)
        lane = lax.broadcasted_iota(jnp.int32, (t_new, HEAD_W), 1)
        for h in range(N_HEADS):
            cols = slice(h * HEAD_W, (h + 1) * HEAD_W)
            r0 = h * rows_per_head
            qf[r0:r0 + t_new, :] = jnp.where(lane < DIFF_D, q[:, cols], 0.0)
            qf[r0 + t_new:r0 + 2 * t_new, :] = jnp.where(lane >= DIFF_D, q[:, cols], 0.0)
        qbf[...] = qf[...].astype(BF16)
        m_ref[...] = jnp.full(m_ref.shape, -jnp.inf, F32)
        acc_ref[...] = jnp.zeros(acc_ref.shape, F32)

    def add_mask(r0, j):
        return mask_ref[r0:r0 + SAMPLE_RB, j * HEAD_W:(j + 1) * HEAD_W]

    def add_last(r0, j):
        return bias_last_ref[r0:r0 + SAMPLE_RB, j * HEAD_W:(j + 1) * HEAD_W]

    def page_update(g, add_fn):
        pg = slice(g * pg_rows, (g + 1) * pg_rows)
        _softmax_tile(s_ref, p_ref, m_ref, alpha_ref.at[g], add_fn, rb=SAMPLE_RB, keep_in_regs=False,
                      cols=(g * pg_rows, pg_rows))
        _accumulate(acc_ref, alpha_ref.at[g], p_ref[:, pg], v1all[pg, :])

    def page_scores(g):
        pg = slice(g * pg_rows, (g + 1) * pg_rows)
        v1all[pg, 0:HEAD_W] = vbuf[slot, g].astype(BF16)
        s_ref[:, pg] = _dot_nt(qbf[...], kbuf[slot, g].astype(BF16))

    page_scores(0)
    for g in range(n_pg - 1):
        page_scores(g + 1)
        page_update(g, add_mask)

    @pl.when(p < last)
    def _():
        page_update(n_pg - 1, add_mask)

    @pl.when(p == last)
    def _():
        page_update(n_pg - 1, add_last)
        knew[0:n_new, :] = kn_ref[0].astype(BF16)
        v1new[0:n_new, 0:HEAD_W] = vn_ref[0].astype(BF16)
        sn_ref[...] = _dot_nt(qbf[...], knew[...])
        _softmax_tile(sn_ref, pn_ref, m_ref, alpha_ref.at[0],
                      lambda r0, j: bias_new_ref[r0:r0 + SAMPLE_RB, :], rb=SAMPLE_RB, keep_in_regs=True)
        _accumulate(acc_ref, alpha_ref.at[0], pn_ref[...], v1new[...])

        lam = _lam(lamp_ref, lam_init)
        for h in range(N_HEADS):
            cols = slice(h * HEAD_W, (h + 1) * HEAD_W)
            r0 = h * rows_per_head
            r1 = r0 + t_new
            o = (acc_ref[r0:r1, 0:HEAD_W] / acc_ref[r0:r1, HEAD_W:2 * HEAD_W]
                 - lam * (acc_ref[r1:r1 + t_new, 0:HEAD_W] / acc_ref[r1:r1 + t_new, HEAD_W:2 * HEAD_W]))
            o_ref[:, cols] = (_rms(o, gn_ref[:, cols]) * (1.0 - lam_init)).astype(o_ref.dtype)


def _diff_sample(proj_a, dk, dv, cache_k, cache_v, page_table, rel_bias, lamp, gn, *, t_new, lam_init, name):
    batch, n_pages = page_table.shape
    n_phys, page = cache_k.shape[0], cache_k.shape[1]
    past = n_pages * page
    rows = N_HEADS * 2 * t_new
    kv_rows = page * N_HEADS
    new_rows = 128
    assert t_new * N_HEADS <= new_rows
    q_head = np.repeat(np.arange(N_HEADS), 2 * t_new)
    q_pos = past + np.tile(np.arange(t_new), 2 * N_HEADS)

    def buckets(n_cols, first_key_pos, n_valid_cols):
        col = np.arange(n_cols)
        k_pos = first_key_pos + col // N_HEADS
        ok = ((col % N_HEADS)[None, :] == q_head[:, None]) & (k_pos[None, :] <= q_pos[:, None])
        ok = ok & (col < n_valid_cols)[None, :]
        return jnp.asarray(np.where(ok, _t5_bucket(q_pos, k_pos), -1).astype(np.int32))

    bkt_last = buckets(kv_rows, past - page, kv_rows)
    bkt_new = buckets(new_rows, past, t_new * N_HEADS)
    n_pg = math.gcd(n_pages, PAGES_PER_STEP)
    n_steps = n_pages // n_pg
    assert PAGE_SLOTS - 1 <= n_steps, "the first batch element must cover the ring's head start"
    grid_spec = pltpu.PrefetchScalarGridSpec(
        num_scalar_prefetch=1,
        grid=(batch, n_steps),
        in_specs=[pl.BlockSpec((t_new, GROUP_W), lambda b, p, pt: (b, N_SECTIONS_A - 1)),
                  pl.BlockSpec(memory_space=pl.ANY),
                  pl.BlockSpec(memory_space=pl.ANY),
                  pl.BlockSpec((1, t_new * N_HEADS, HEAD_W), lambda b, p, pt: (b, 0, 0)),
                  pl.BlockSpec((1, t_new * N_HEADS, HEAD_W), lambda b, p, pt: (b, 0, 0)),
                  pl.BlockSpec((rows, kv_rows), lambda b, p, pt: (0, 0)),
                  pl.BlockSpec((rows, new_rows), lambda b, p, pt: (0, 0)),
                  pl.BlockSpec(memory_space=pltpu.SMEM),
                  pl.BlockSpec((4, DIFF_D), lambda b, p, pt: (0, 0)),
                  pl.BlockSpec((1, GROUP_W), lambda b, p, pt: (0, 0))],
        out_specs=pl.BlockSpec((t_new, GROUP_W), lambda b, p, pt: (b, 0)),
        scratch_shapes=[pltpu.VMEM((rows, HEAD_W), F32), pltpu.VMEM((rows, HEAD_W), BF16),
                        pltpu.VMEM((n_pg * kv_rows, 2 * HEAD_W), BF16),
                        pltpu.VMEM((new_rows, HEAD_W), BF16), pltpu.VMEM((new_rows, 2 * HEAD_W), BF16),
                        pltpu.VMEM((rows, n_pg * kv_rows), F32), pltpu.VMEM((rows, n_pg * kv_rows), BF16),
                        pltpu.VMEM((rows, new_rows), F32), pltpu.VMEM((rows, new_rows), BF16),
                        pltpu.VMEM((rows, HEAD_W), F32), pltpu.VMEM((n_pg, rows, HEAD_W), F32),
                        pltpu.VMEM((rows, 2 * HEAD_W), F32),
                        pltpu.VMEM((rows, kv_rows), F32), pltpu.VMEM((rows, kv_rows), F32),
                        pltpu.VMEM((rows, new_rows), F32),
                        pltpu.VMEM((PAGE_SLOTS, n_pg, kv_rows, HEAD_W), F32),
                        pltpu.VMEM((PAGE_SLOTS, n_pg, kv_rows, HEAD_W), F32),
                        pltpu.SemaphoreType.DMA((2, PAGE_SLOTS))],
    )
    return pl.pallas_call(
        functools.partial(_diff_sample_kernel, t_new=t_new, lam_init=lam_init, n_pg=n_pg, n_steps=n_steps,
                          n_batch=batch),
        grid_spec=grid_spec,
        out_shape=jax.ShapeDtypeStruct((batch * t_new, GROUP_W), F32),
        compiler_params=_cparams(("arbitrary", "arbitrary")),
        name=name,
    )(page_table, proj_a, cache_k.reshape(n_phys, kv_rows, HEAD_W), cache_v.reshape(n_phys, kv_rows, HEAD_W),
      dk.reshape(batch, t_new * N_HEADS, HEAD_W), dv.reshape(batch, t_new * N_HEADS, HEAD_W),
      bkt_last, bkt_new, rel_bias, lamp, gn)


def _mix_out_kernel(ret_ref, diff_ref, w_ret_ref, w_diff_ref, gain_ref, x_ref, gain_q_ref, wq_ref, *refs, has_mem):
    if has_mem:
        mk_ref, mv_ref, x1_ref, o_ref = refs
    else:
        x1_ref, o_ref = refs
    y = _dot(ret_ref[...].astype(BF16), w_ret_ref[...]) + _dot(diff_ref[...].astype(BF16), w_diff_ref[...])
    x1 = x_ref[...] + _rms(y, gain_ref[...])
    x1_ref[...] = x1
    q = _dot(_rms(x1, gain_q_ref[...]).astype(BF16), wq_ref[...])
    if not has_mem:
        o_ref[...] = q.astype(o_ref.dtype)
        return
    for h in range(N_XHEADS):
        cols = slice(h * HEAD_W, (h + 1) * HEAD_W)
        s = _dot_nt(q[:, cols].astype(BF16), mk_ref[0, :, cols].astype(BF16)) * (HEAD_W ** -0.5)
        e = jnp.exp(s - jnp.max(s, axis=-1, keepdims=True))
        p = e / jnp.sum(e, axis=-1, keepdims=True)
        o_ref[:, cols] = _dot(p.astype(BF16), mv_ref[0, :, cols].astype(BF16)).astype(o_ref.dtype)


def _mix_out(ret_o, diff_o, w_out, gain, x, gain_q, w_xq, mem, *, tm, rows_per_batch, out_dtype, name):
    m, n = x.shape
    ka = ret_o.shape[1]
    xw = w_xq.shape[1]
    in_specs = [pl.BlockSpec((tm, ka), lambda i: (i, 0)), pl.BlockSpec((tm, ka), lambda i: (i, 0)),
                pl.BlockSpec((ka, n), lambda i: (0, 0)), pl.BlockSpec((ka, n), lambda i: (1, 0)),
                pl.BlockSpec((1, n), lambda i: (0, 0)), pl.BlockSpec((tm, n), lambda i: (i, 0)),
                pl.BlockSpec((1, n), lambda i: (0, 0)), pl.BlockSpec((n, xw), lambda i: (0, 0))]
    args = [ret_o, diff_o, w_out, w_out, gain, x, gain_q, w_xq]
    if mem is not None:
        assert rows_per_batch % tm == 0
        per = rows_per_batch // tm
        n_mem = mem[0].shape[1]
        in_specs += [pl.BlockSpec((1, n_mem, xw), lambda i: (i // per, 0, 0))] * 2
        args += list(mem)
    return pl.pallas_call(
        functools.partial(_mix_out_kernel, has_mem=mem is not None),
        grid=(m // tm,),
        in_specs=in_specs,
        out_specs=[pl.BlockSpec((tm, n), lambda i: (i, 0)), pl.BlockSpec((tm, xw), lambda i: (i, 0))],
        out_shape=[jax.ShapeDtypeStruct((m, n), F32), jax.ShapeDtypeStruct((m, xw), out_dtype)],
        compiler_params=_cparams(("parallel",)),
        name=name,
    )(*args)


def _cross_interleaved_kernel(q_ref, mk_ref, mv_ref, o_ref):
    tq = q_ref.shape[1]
    q = q_ref[0]
    qs = jnp.concatenate([q[:, h * HEAD_W:(h + 1) * HEAD_W] for h in range(N_XHEADS)], axis=0).astype(BF16)
    s = _dot_nt(qs, mk_ref[0].astype(BF16)) * (HEAD_W ** -0.5)
    row_head = lax.broadcasted_iota(jnp.int32, s.shape, 0) // tq
    col_head = lax.broadcasted_iota(jnp.int32, s.shape, 1) % N_XHEADS
    s = jnp.where(row_head == col_head, s, NEG_BIG)
    e = jnp.exp(s - jnp.max(s, axis=-1, keepdims=True))
    p = e / jnp.sum(e, axis=-1, keepdims=True)
    o = _dot(p.astype(BF16), mv_ref[0].astype(BF16))
    for h in range(N_XHEADS):
        o_ref[0, :, h * HEAD_W:(h + 1) * HEAD_W] = o[h * tq:(h + 1) * tq, :].astype(o_ref.dtype)


def _cross_attend_interleaved(q, mk, mv, *, name):
    b, t, w = q.shape
    rows = mk.shape[1]
    return pl.pallas_call(
        _cross_interleaved_kernel,
        grid=(b,),
        in_specs=[pl.BlockSpec((1, t, w), lambda i: (i, 0, 0)),
                  pl.BlockSpec((1, rows, HEAD_W), lambda i: (i, 0, 0)),
                  pl.BlockSpec((1, rows, HEAD_W), lambda i: (i, 0, 0))],
        out_specs=pl.BlockSpec((1, t, w), lambda i: (i, 0, 0)),
        out_shape=jax.ShapeDtypeStruct((b, t, w), q.dtype),
        compiler_params=_cparams(("parallel",)),
        name=name,
    )(q, mk, mv)


def _xo_ffn_kernel(x_ref, o_ref, wxo_ref, gx_ref, gpre_ref, wg_ref, wu_ref, wd_ref, gpost_ref, y_ref, *rest):
    x2_ref, h_ref, acc_ref = rest[-3:]
    copies = rest[:-3]
    f = pl.program_id(1)

    def weight(w_ref, i):
        w = w_ref[...]
        if copies:
            w = w.astype(BF16)
            copies[i][...] = w
        return w

    @pl.when(f == 0)
    def _():
        x2 = x_ref[...] + _rms(_dot(o_ref[...].astype(BF16), weight(wxo_ref, 0)), gx_ref[...])
        x2_ref[...] = x2
        h_ref[...] = _rms(x2, gpre_ref[...]).astype(BF16)
        acc_ref[...] = jnp.zeros(acc_ref.shape, F32)

    h = h_ref[...]
    g = _dot(h, weight(wg_ref, 1))
    a = (g * jax.nn.sigmoid(g)) * _dot(h, weight(wu_ref, 2))
    acc_ref[...] += _dot(a.astype(BF16), weight(wd_ref, 3))

    @pl.when(f == pl.num_programs(1) - 1)
    def _():
        y_ref[...] = x2_ref[...] + _rms(acc_ref[...], gpost_ref[...])


def _xo_ffn(x, o, w_xo, gx, gpre, wg, wu, wd, gpost, *, tm, tf, name):
    m, d = x.shape
    xw = o.shape[1]
    dff = wg.shape[1]
    emit_w = wg.dtype != BF16
    assert not emit_w or m == tm, "each weight tile must be visited once to be copied out"
    w_specs = [pl.BlockSpec((xw, d), lambda i, f: (0, 0)),
               pl.BlockSpec((d, tf), lambda i, f: (0, f)),
               pl.BlockSpec((d, tf), lambda i, f: (0, f)),
               pl.BlockSpec((tf, d), lambda i, f: (f, 0))]
    out_specs = [pl.BlockSpec((tm, d), lambda i, f: (i, 0))]
    out_shape = [jax.ShapeDtypeStruct((m, d), F32)]
    if emit_w:
        out_specs += w_specs
        out_shape += [jax.ShapeDtypeStruct(a.shape, BF16) for a in (w_xo, wg, wu, wd)]
    out = pl.pallas_call(
        _xo_ffn_kernel,
        grid=(m // tm, dff // tf),
        in_specs=[pl.BlockSpec((tm, d), lambda i, f: (i, 0)),
                  pl.BlockSpec((tm, xw), lambda i, f: (i, 0)),
                  w_specs[0],
                  pl.BlockSpec((1, d), lambda i, f: (0, 0)),
                  pl.BlockSpec((1, d), lambda i, f: (0, 0)),
                  w_specs[1], w_specs[2], w_specs[3],
                  pl.BlockSpec((1, d), lambda i, f: (0, 0))],
        out_specs=out_specs,
        out_shape=out_shape,
        scratch_shapes=[pltpu.VMEM((tm, d), F32), pltpu.VMEM((tm, d), BF16), pltpu.VMEM((tm, d), F32)],
        compiler_params=_cparams(("parallel", "arbitrary")),
        name=name,
    )(x, o, w_xo, gx, gpre, wg, wu, wd, gpost)
    return out if emit_w else out[0]


def _rope_tables(pos):
    inv_freq = ROPE_BASE ** (-jnp.arange(0, HEAD_W, 2, dtype=F32) / HEAD_W)
    ang = pos.astype(F32)[:, None] * inv_freq[None, :]
    cos = jnp.repeat(jnp.cos(ang), 2, axis=-1)
    sin = jnp.sin(ang)
    return cos, jnp.stack([-sin, sin], axis=-1).reshape(cos.shape)


def _layer(x, pos, s0, mem_k, mem_v, paged, w, *, batch, seq, lam_init, tag):
    m = batch * seq
    big = paged is None
    tm = 512 if big else m
    w = dict(w)
    proj_a, dk, dv, *w_in_copy = _in_proj(x, w["n_pre_mix"], w["w_in"], tm=1024 if big else m, tn=512,
                                          name=f"in_proj_{tag}")
    if w_in_copy:
        w["w_in"] = w_in_copy[0]
    cos, sin = _rope_tables(pos)
    ret_o, ret_s = _retention(proj_a, cos, sin, w["lg"], w["ret_gn"], s0, batch=batch, seq=seq,
                              heads=1 if big else N_HEADS, out_dtype=BF16 if big else F32, name=f"retention_{tag}")
    if big:
        diff_o = _diff_prompt(proj_a, dk, dv, w["rel_bias"], w["lamp"], w["diff_gn"], batch=batch, seq=seq,
                              t=512, lam_init=lam_init, name=f"diff_attn_{tag}")
    else:
        diff_o = _diff_sample(proj_a, dk, dv, *paged, w["rel_bias"], w["lamp"], w["diff_gn"], t_new=seq,
                              lam_init=lam_init, name=f"diff_attn_{tag}")
    xw = N_XHEADS * HEAD_W
    x, o = _mix_out(ret_o, diff_o, w["w_out"], w["n_post_mix"], x, w["n_pre_x"], w["w_xq"],
                    (mem_k, mem_v) if big else None, tm=tm, rows_per_batch=seq,
                    out_dtype=BF16 if big else F32, name=f"mix_out_{tag}")
    if not big:
        o = _cross_attend_interleaved(o.reshape(batch, seq, xw), mem_k, mem_v, name=f"cross_{tag}").reshape(m, xw)
    out = _xo_ffn(x, o, w["w_xo"], w["n_post_x"], w["n_pre_ffn"], w["w_gate"], w["w_up"], w["w_down"],
                  w["n_post_ffn"], tm=tm, tf=512 if big else 256, name=f"ffn_{tag}")
    if isinstance(out, (list, tuple)):
        x, w["w_xo"], w["w_gate"], w["w_up"], w["w_down"] = out
    else:
        x = out
    return x, dk, dv, ret_s, w


def kernel(x_prompt, x_sample, cache_k, cache_v, state_ret, cache_mem_k, cache_mem_v, page_table, mem_prompt,
           rel_bias, norm_pre_mix, norm_post_mix, norm_pre_x, norm_post_x, norm_pre_ffn, norm_post_ffn, norm_mem,
           w_in, w_out, ret_gn, diff_gn, lam_q1, lam_k1, lam_q2, lam_k2,
           w_xq, w_xk, w_xv, w_xo, w_gate, w_up, w_down):
    b_p, t_p, d = x_prompt.shape
    b_s, t_s, _ = x_sample.shape
    depth = w_in.shape[0]
    n_mem = mem_prompt.shape[1]
    xw = N_XHEADS * HEAD_W
    past_len = page_table.shape[1] * cache_k.shape[2]
    pos_p = jnp.arange(t_p)
    pos_s = past_len + jnp.arange(t_s)
    log_g = jnp.log1p(-(2.0 ** (-5.0 - jnp.arange(N_HEADS, dtype=F32))))
    lg = jnp.broadcast_to(log_g[:, None, None], (N_HEADS, 8, HEAD_W))
    xp = x_prompt.reshape(b_p * t_p, d)
    xs = x_sample.reshape(b_s * t_s, d)
    outs = [[] for _ in range(8)]
    for l in range(depth):
        lam_init = 0.8 - 0.6 * math.exp(-0.3 * l)
        w = {
            "n_pre_mix": norm_pre_mix[l][None], "n_post_mix": norm_post_mix[l][None],
            "n_pre_x": norm_pre_x[l][None], "n_post_x": norm_post_x[l][None],
            "n_pre_ffn": norm_pre_ffn[l][None], "n_post_ffn": norm_post_ffn[l][None],
            "w_in": w_in[l], "w_out": w_out[l].astype(BF16),
            "ret_gn": ret_gn[l][None], "diff_gn": diff_gn[l][None],
            "lamp": jnp.stack([lam_q1[l], lam_k1[l], lam_q2[l], lam_k2[l]]),
            "w_xq": w_xq[l].astype(BF16), "w_xo": w_xo[l],
            "w_gate": w_gate[l], "w_up": w_up[l], "w_down": w_down[l],
            "rel_bias": rel_bias, "lg": lg,
        }
        mem_flat = mem_prompt.reshape(b_p * n_mem, d)
        mk_p, mv_p = _norm_matmul_pair(mem_flat, norm_mem[l][None], w_xk[l], w_xv[l], name="mem_kv")
        s0 = jnp.zeros((b_p, N_HEADS, HEAD_W, HEAD_W), F32)
        xs, dk_s, dv_s, s_s, w = _layer(xs, pos_s, state_ret[l],
                                        cache_mem_k[l].reshape(b_s, n_mem * N_XHEADS, HEAD_W),
                                        cache_mem_v[l].reshape(b_s, n_mem * N_XHEADS, HEAD_W),
                                        (cache_k[l], cache_v[l], page_table), w,
                                        batch=b_s, seq=t_s, lam_init=lam_init, tag="sample")
        xp, dk_p, dv_p, s_p, _ = _layer(xp, pos_p, s0, mk_p.reshape(b_p, n_mem, xw), mv_p.reshape(b_p, n_mem, xw),
                                        None, w, batch=b_p, seq=t_p, lam_init=lam_init, tag="prompt")
        per_layer = (dk_p.reshape(b_p, t_p, N_HEADS, HEAD_W), dv_p.reshape(b_p, t_p, N_HEADS, HEAD_W), s_p,
                     mk_p.reshape(b_p, n_mem, N_XHEADS, HEAD_W), mv_p.reshape(b_p, n_mem, N_XHEADS, HEAD_W),
                     dk_s.reshape(b_s, t_s, N_HEADS, HEAD_W), dv_s.reshape(b_s, t_s, N_HEADS, HEAD_W), s_s)
        for acc, val in zip(outs, per_layer):
            acc.append(val)
    stacked = [jnp.stack(o) for o in outs]
    return (xp.reshape(b_p, t_p, d), xs.reshape(b_s, t_s, d), *stacked)
```

```python
import functools
import math

import jax
import jax.numpy as jnp
import numpy as np
from jax import lax
from jax.experimental import pallas as pl
from jax.experimental.pallas import tpu as pltpu

F32 = jnp.float32
BF16 = jnp.bfloat16

D_MODEL = 2048
N_HEADS = 8
HEAD_W = 128
GROUP_W = N_HEADS * HEAD_W
DIFF_D = 64
ROPE_BASE = 10000.0
RET_CHUNK = 128
N_BUCKETS = 32
MAX_EXACT = N_BUCKETS // 2
MAX_DISTANCE = 128
BIAS_BAND = 128
N_XHEADS = 4
RMS_EPS = 1e-6
NEG_BIG = -1e30
LOG2E = math.log2(math.e)
PROMPT_RB = 32
SAMPLE_RB = 16
PAGES_PER_STEP = 8
CROSS_BATCH = 4
PAGE_SLOTS = 4
RET_UNROLL = 8
N_SECTIONS_A = 5
VMEM_LIMIT = 56 * 1024 * 1024


def _cparams(sem):
    return pltpu.CompilerParams(dimension_semantics=sem, vmem_limit_bytes=VMEM_LIMIT)


def _rms(x, gain):
    return x * lax.rsqrt(jnp.mean(x * x, axis=-1, keepdims=True) + RMS_EPS) * gain


def _dot(a, b):
    return jnp.dot(a, b, preferred_element_type=F32)


def _dot_nt(a, b):
    return lax.dot_general(a, b, (((1,), (1,)), ((), ())), preferred_element_type=F32)


def _dot_tn(a, b):
    return lax.dot_general(a, b, (((0,), (0,)), ((), ())), preferred_element_type=F32)


def _norm_matmul_pair_kernel(x_ref, g_ref, wa_ref, wb_ref, oa_ref, ob_ref):
    h = _rms(x_ref[...], g_ref[...]).astype(BF16)
    oa_ref[...] = _dot(h, wa_ref[...].astype(BF16))
    ob_ref[...] = _dot(h, wb_ref[...].astype(BF16))


def _norm_matmul_pair(x, gain, wa, wb, *, name):
    m, k = x.shape
    n = wa.shape[1]
    full = lambda shape: pl.BlockSpec(shape, lambda i: (0, 0))
    return pl.pallas_call(
        _norm_matmul_pair_kernel,
        grid=(1,),
        in_specs=[full((m, k)), full((1, k)), full((k, n)), full((k, n))],
        out_specs=[full((m, n)), full((m, n))],
        out_shape=[jax.ShapeDtypeStruct((m, n), F32)] * 2,
        compiler_params=_cparams(("arbitrary",)),
        name=name,
    )(x, gain, wa, wb)


def _in_proj_kernel(x_ref, g_ref, w_ref, a_ref, dk_ref, dv_ref, *rest, n_a, n_g):
    j = pl.program_id(1)
    h_ref = rest[-1]

    @pl.when(j == 0)
    def _():
        h_ref[...] = _rms(x_ref[...], g_ref[...]).astype(BF16)

    w = w_ref[...]
    if len(rest) == 2:
        w = w.astype(BF16)
        rest[0][...] = w
    res = _dot(h_ref[...], w)

    @pl.when(j < n_a)
    def _():
        a_ref[...] = res

    @pl.when(jnp.logical_and(j >= n_a, j < n_a + n_g))
    def _():
        dk_ref[...] = res

    @pl.when(j >= n_a + n_g)
    def _():
        dv_ref[...] = res


def _in_proj(x, gain, w, *, tm, tn, name):
    m, k = x.shape
    n = w.shape[1]
    n_g = GROUP_W // tn
    n_a = N_SECTIONS_A * n_g
    emit_w = w.dtype != BF16
    assert not emit_w or m == tm, "each weight tile must be visited once to be copied out"
    out_specs = [pl.BlockSpec((tm, tn), lambda i, j: (i, jnp.minimum(j, n_a - 1))),
                 pl.BlockSpec((tm, tn), lambda i, j: (i, jnp.clip(j - n_a, 0, n_g - 1))),
                 pl.BlockSpec((tm, tn), lambda i, j: (i, jnp.clip(j - n_a - n_g, 0, n_g - 1)))]
    out_shape = [jax.ShapeDtypeStruct((m, N_SECTIONS_A * GROUP_W), F32),
                 jax.ShapeDtypeStruct((m, GROUP_W), F32),
                 jax.ShapeDtypeStruct((m, GROUP_W), F32)]
    if emit_w:
        out_specs.append(pl.BlockSpec((k, tn), lambda i, j: (0, j)))
        out_shape.append(jax.ShapeDtypeStruct((k, n), BF16))
    return pl.pallas_call(
        functools.partial(_in_proj_kernel, n_a=n_a, n_g=n_g),
        grid=(m // tm, n // tn),
        in_specs=[pl.BlockSpec((tm, k), lambda i, j: (i, 0)),
                  pl.BlockSpec((1, k), lambda i, j: (0, 0)),
                  pl.BlockSpec((k, tn), lambda i, j: (0, j))],
        out_specs=out_specs,
        out_shape=out_shape,
        scratch_shapes=[pltpu.VMEM((tm, k), BF16)],
        compiler_params=_cparams(("parallel", "arbitrary")),
        name=name,
    )(x, gain, w)


def _rotate(x, cos, sin_signed):
    lane = lax.broadcasted_iota(jnp.int32, x.shape, 1)
    nxt = pltpu.roll(x, HEAD_W - 1, 1)
    prv = pltpu.roll(x, 1, 1)
    return x * cos + jnp.where(lane % 2 == 0, nxt, prv) * sin_signed


def _retention_kernel(q_ref, k_ref, v_ref, g_ref, cos_ref, sin_ref, lg_ref, gn_ref, s0_ref,
                      o_ref, sf_ref, kr_ref, st_ref, *, chunk, n_chunks, heads):
    rows_mm = RET_CHUNK
    ri = lax.broadcasted_iota(jnp.int32, (rows_mm, rows_mm), 0)
    ci = lax.broadcasted_iota(jnp.int32, (rows_mm, rows_mm), 1)
    rel = (ri - ci).astype(F32)
    idx = lax.broadcasted_iota(jnp.int32, (rows_mm, 1), 0).astype(F32)

    def pad(a):
        if chunk == rows_mm:
            return a
        return jnp.concatenate([a, jnp.zeros((rows_mm - chunk, a.shape[1]), a.dtype)], axis=0)

    for hh in range(heads):
        cols = slice(hh * HEAD_W, (hh + 1) * HEAD_W)
        lg = lg_ref[hh, 0:1, 0:1]
        intra = jnp.where(rel >= 0, jnp.exp(lg * jnp.maximum(rel, 0.0)), 0.0)
        q_dec = jnp.exp(lg * (idx + 1.0))
        k_dec = jnp.exp(lg * jnp.maximum(chunk - 1.0 - idx, 0.0))
        chunk_dec = jnp.exp(lg * float(chunk))
        gn = gn_ref[:, cols]

        def chunk_rows(c):
            if n_chunks == 1:
                return slice(0, chunk), slice(0, rows_mm)
            return (pl.ds(pl.multiple_of(c * chunk, chunk), chunk),
                    pl.ds(pl.multiple_of(c * rows_mm, rows_mm), rows_mm))

        def state_step(c, s, hh=hh, cols=cols, k_dec=k_dec, chunk_dec=chunk_dec):
            rows, rows_p = chunk_rows(c)
            k = pad(_rotate(k_ref[rows, cols], cos_ref[rows, :], sin_ref[rows, :]) * (HEAD_W ** -0.5))
            kr_ref[rows_p, cols] = k.astype(BF16)
            st_ref[hh, c] = s.astype(BF16)
            vb = pad(v_ref[rows, cols]).astype(BF16)
            return s * chunk_dec + _dot_tn((k * k_dec).astype(BF16), vb)

        def output_step(c, carry, hh=hh, cols=cols, intra=intra, q_dec=q_dec, gn=gn):
            rows, rows_p = chunk_rows(c)
            qb = _rotate(q_ref[rows, cols], cos_ref[rows, :], sin_ref[rows, :]).astype(BF16)
            vb = pad(v_ref[rows, cols]).astype(BF16)
            scores = _dot_nt(qb, kr_ref[rows_p, cols]) * intra[0:chunk, :]
            o = _dot(scores.astype(BF16), vb) + _dot(qb, st_ref[hh, c]) * q_dec[0:chunk, :]
            gate = g_ref[rows, cols]
            y = _rms(o, gn)
            o_ref[rows, cols] = ((gate * jax.nn.sigmoid(gate)) * y).astype(o_ref.dtype)
            return carry

        s0 = s0_ref[0, hh]
        if n_chunks == 1:
            sf_ref[0, hh] = state_step(0, s0)
            output_step(0, 0)
        else:
            unroll = math.gcd(n_chunks, RET_UNROLL)
            sf_ref[0, hh] = lax.fori_loop(0, n_chunks, state_step, s0, unroll=unroll)
            lax.fori_loop(0, n_chunks, output_step, 0, unroll=unroll)


def _retention(proj_a, cos, sin, lg, gn, s0, *, batch, seq, heads, out_dtype, name):
    chunk = math.gcd(seq, RET_CHUNK)
    w = heads * HEAD_W
    n_hb = N_HEADS // heads

    def sec(s):
        return pl.BlockSpec((seq, w), lambda b, h, s=s: (b, s * n_hb + h))

    return pl.pallas_call(
        functools.partial(_retention_kernel, chunk=chunk, n_chunks=seq // chunk, heads=heads),
        grid=(batch, n_hb),
        in_specs=[sec(0), sec(1), sec(2), sec(3),
                  pl.BlockSpec((seq, HEAD_W), lambda b, h: (0, 0)),
                  pl.BlockSpec((seq, HEAD_W), lambda b, h: (0, 0)),
                  pl.BlockSpec((heads, 8, HEAD_W), lambda b, h: (h, 0, 0)),
                  pl.BlockSpec((1, w), lambda b, h: (0, h)),
                  pl.BlockSpec((1, heads, HEAD_W, HEAD_W), lambda b, h: (b, h, 0, 0))],
        out_specs=[pl.BlockSpec((seq, w), lambda b, h: (b, h)),
                   pl.BlockSpec((1, heads, HEAD_W, HEAD_W), lambda b, h: (b, h, 0, 0))],
        out_shape=[jax.ShapeDtypeStruct((batch * seq, GROUP_W), out_dtype),
                   jax.ShapeDtypeStruct((batch, N_HEADS, HEAD_W, HEAD_W), F32)],
        scratch_shapes=[pltpu.VMEM((seq // chunk * RET_CHUNK, w), BF16),
                        pltpu.VMEM((heads, seq // chunk, HEAD_W, HEAD_W), BF16)],
        compiler_params=_cparams(("parallel", "parallel")),
        name=name,
    )(proj_a, proj_a, proj_a, proj_a, cos, sin, lg, gn, s0)


def _t5_bucket(qpos, kpos):
    n = np.maximum(qpos[:, None] - kpos[None, :], 0)
    nf = np.maximum(n, 1).astype(np.float32)
    large = MAX_EXACT + (np.log(nf / np.float32(MAX_EXACT)) / np.float32(math.log(MAX_DISTANCE / MAX_EXACT))
                         * np.float32(N_BUCKETS - MAX_EXACT)).astype(np.int32)
    large = np.minimum(large, N_BUCKETS - 1)
    return np.where(n < MAX_EXACT, n, large).astype(np.int32)


def _bias_from_buckets(bkt, rb_ref, h):
    far = rb_ref[N_BUCKETS - 1, h]
    b = jnp.zeros(bkt.shape, F32)
    for u in range(N_BUCKETS - 1):
        b = jnp.where(bkt == u, (rb_ref[u, h] - far) * LOG2E, b)
    return b


def _lam(lamp_ref, lam_init):
    e1 = jnp.exp(jnp.sum(lamp_ref[0:1, :] * lamp_ref[1:2, :], axis=-1, keepdims=True))
    e2 = jnp.exp(jnp.sum(lamp_ref[2:3, :] * lamp_ref[3:4, :], axis=-1, keepdims=True))
    return e1 - e2 + lam_init


def _softmax_tile(s_ref, p_ref, m_ref, alpha_ref, add_fn, *, rb, keep_in_regs, cols=None):
    n_rows = s_ref.shape[0]
    col0, n_cols = (0, s_ref.shape[1]) if cols is None else cols
    n_slabs = n_cols // HEAD_W
    for r0 in range(0, n_rows, rb):
        r = slice(r0, r0 + rb)
        m_prev = m_ref[r, :]
        vals, part = [], None
        for j in range(n_slabs):
            c = slice(col0 + j * HEAD_W, col0 + (j + 1) * HEAD_W)
            s = s_ref[r, c]
            if add_fn is not None:
                s = s + add_fn(r0, j)
                if not keep_in_regs:
                    s_ref[r, c] = s
            if keep_in_regs:
                vals.append(s)
            part = s if part is None else jnp.maximum(part, s)
        m_new = jnp.maximum(m_prev, jnp.max(part, axis=-1, keepdims=True))
        for j in range(n_slabs):
            c = slice(col0 + j * HEAD_W, col0 + (j + 1) * HEAD_W)
            s = vals[j] if keep_in_regs else s_ref[r, c]
            p_ref[r, c] = jnp.exp2(s - m_new).astype(BF16)
        alpha_ref[r, :] = jnp.exp2(m_prev - m_new)
        m_ref[r, :] = m_new


def _accumulate(acc_ref, alpha_ref, p, v1):
    a = alpha_ref[...]
    acc_ref[...] = jnp.concatenate([a, a], axis=1) * acc_ref[...] + _dot(p, v1)


def _diff_prompt_kernel(q_ref, k_ref, v_ref, bkt_ref, rb_ref, lamp_ref, gn_ref, o_ref,
                        kbf, v1, qs, s_a, s_b, p_a, p_b, m_ref, alpha_a, alpha_b, acc_ref, bias_ref,
                        *, t, lam_init):
    h = pl.program_id(0)
    b = pl.program_id(1)
    qi = pl.program_id(2)

    @pl.when(qi == 0)
    def _():
        kbf[...] = k_ref[...].astype(BF16)
        v1[:, 0:HEAD_W] = v_ref[...].astype(BF16)
        v1[:, HEAD_W:2 * HEAD_W] = jnp.ones((v1.shape[0], HEAD_W), BF16)

    @pl.when(jnp.logical_and(b == 0, qi == 0))
    def _():
        ri = lax.broadcasted_iota(jnp.int32, (t, t), 0)
        ci = lax.broadcasted_iota(jnp.int32, (t, t), 1)
        bias_ref[0] = jnp.where(ci <= ri, 0.0, NEG_BIG)
        bias_ref[1] = jnp.zeros((t, t), F32)
        bias_ref[2] = jnp.zeros((t, t), F32)
        bri = lax.broadcasted_iota(jnp.int32, (BIAS_BAND, BIAS_BAND), 0)
        bci = lax.broadcasted_iota(jnp.int32, (BIAS_BAND, BIAS_BAND), 1)
        for rk in range(t // BIAS_BAND):
            r = slice(rk * BIAS_BAND, (rk + 1) * BIAS_BAND)
            bias_ref[0, r, r] = jnp.where(bci <= bri, _bias_from_buckets(bkt_ref[0, r, r], rb_ref, h), NEG_BIG)
            if rk > 0:
                c = slice((rk - 1) * BIAS_BAND, rk * BIAS_BAND)
                bias_ref[0, r, c] = _bias_from_buckets(bkt_ref[0, r, c], rb_ref, h)
        r, c = slice(0, BIAS_BAND), slice(t - BIAS_BAND, t)
        bias_ref[1, r, c] = _bias_from_buckets(bkt_ref[1, r, c], rb_ref, h)

    q = q_ref[...] * (DIFF_D ** -0.5 * LOG2E)
    lane = lax.broadcasted_iota(jnp.int32, q.shape, 1)
    qs[0:t, :] = jnp.where(lane < DIFF_D, q, 0.0).astype(BF16)
    qs[t:2 * t, :] = jnp.where(lane >= DIFF_D, q, 0.0).astype(BF16)
    m_ref[...] = jnp.full(m_ref.shape, -jnp.inf, F32)
    acc_ref[...] = jnp.zeros(acc_ref.shape, F32)

    def scores(k, s_buf):
        rows = pl.ds(pl.multiple_of(jnp.minimum(k, qi) * t, t), t)
        s_buf[...] = _dot_nt(qs[...], kbf[rows, :])

    def update(k, s_buf, p_buf, alpha_buf):
        which = jnp.where(k == qi, 0, jnp.where(k == qi - 1, 1, 2))

        def add_fn(r0, j):
            q0 = r0 % t
            return bias_ref[which, q0:q0 + PROMPT_RB, j * HEAD_W:(j + 1) * HEAD_W]

        _softmax_tile(s_buf, p_buf, m_ref, alpha_buf, add_fn, rb=PROMPT_RB, keep_in_regs=True)
        rows = pl.ds(pl.multiple_of(k * t, t), t)
        _accumulate(acc_ref, alpha_buf, p_buf[...], v1[rows, :])

    n_tiles = qi + 1
    scores(0, s_a)

    def pair(j, carry):
        k0 = 2 * j
        scores(k0 + 1, s_b)
        update(k0, s_a, p_a, alpha_a)
        scores(k0 + 2, s_a)
        update(k0 + 1, s_b, p_b, alpha_b)
        return carry

    lax.fori_loop(0, n_tiles // 2, pair, 0)

    @pl.when(n_tiles % 2 == 1)
    def _():
        update(qi, s_a, p_a, alpha_a)

    lam = _lam(lamp_ref, lam_init)
    o = (acc_ref[0:t, 0:HEAD_W] / acc_ref[0:t, HEAD_W:2 * HEAD_W]
         - lam * (acc_ref[t:2 * t, 0:HEAD_W] / acc_ref[t:2 * t, HEAD_W:2 * HEAD_W]))
    o_ref[...] = (_rms(o, gn_ref[...]) * (1.0 - lam_init)).astype(o_ref.dtype)


def _diff_prompt(proj_a, dk, dv, rel_bias, lamp, gn, *, batch, seq, t, lam_init, name):
    nq = seq // t
    assert t % BIAS_BAND == 0 and BIAS_BAND >= MAX_DISTANCE
    pos = np.arange(t)
    bkt = jnp.asarray(np.stack([_t5_bucket(pos, pos), _t5_bucket(pos + t, pos)]))
    q_sec = (N_SECTIONS_A - 1) * N_HEADS
    return pl.pallas_call(
        functools.partial(_diff_prompt_kernel, t=t, lam_init=lam_init),
        grid=(N_HEADS, batch, nq),
        in_specs=[pl.BlockSpec((t, HEAD_W), lambda h, b, i: (b * nq + i, q_sec + h)),
                  pl.BlockSpec((seq, HEAD_W), lambda h, b, i: (b, h)),
                  pl.BlockSpec((seq, HEAD_W), lambda h, b, i: (b, h)),
                  pl.BlockSpec((2, t, t), lambda h, b, i: (0, 0, 0)),
                  pl.BlockSpec(memory_space=pltpu.SMEM),
                  pl.BlockSpec((4, DIFF_D), lambda h, b, i: (0, 0)),
                  pl.BlockSpec((1, HEAD_W), lambda h, b, i: (0, h))],
        out_specs=pl.BlockSpec((t, HEAD_W), lambda h, b, i: (b * nq + i, h)),
        out_shape=jax.ShapeDtypeStruct((batch * seq, GROUP_W), BF16),
        scratch_shapes=[pltpu.VMEM((seq, HEAD_W), BF16), pltpu.VMEM((seq, 2 * HEAD_W), BF16),
                        pltpu.VMEM((2 * t, HEAD_W), BF16),
                        pltpu.VMEM((2 * t, t), F32), pltpu.VMEM((2 * t, t), F32),
                        pltpu.VMEM((2 * t, t), BF16), pltpu.VMEM((2 * t, t), BF16),
                        pltpu.VMEM((2 * t, HEAD_W), F32), pltpu.VMEM((2 * t, HEAD_W), F32),
                        pltpu.VMEM((2 * t, HEAD_W), F32), pltpu.VMEM((2 * t, 2 * HEAD_W), F32),
                        pltpu.VMEM((3, t, t), F32)],
        compiler_params=_cparams(("arbitrary", "arbitrary", "arbitrary")),
        name=name,
    )(proj_a, dk, dv, bkt, rel_bias, lamp, gn)


def _diff_sample_kernel(pt_ref, q_ref, *refs, t_new, lam_init, n_pg, n_steps, n_batch):
    (ck_hbm, cv_hbm, kn_ref, vn_ref, bkt_last_ref, bkt_new_ref, rb_ref, lamp_ref, gn_ref, o_ref,
     qf, qbf, v1all, knew, v1new, s_ref, p_ref, sn_ref, pn_ref, m_ref, alpha_ref, acc_ref,
     mask_ref, bias_last_ref, bias_new_ref, kbuf, vbuf, sem) = refs
    b = pl.program_id(0)
    p = pl.program_id(1)
    last = n_steps - 1
    rows_per_head = 2 * t_new
    n_new = t_new * N_HEADS
    pg_rows = mask_ref.shape[1]

    def page_copies(bb, pp, slot):
        out = []
        for g in range(n_pg):
            page = pt_ref[bb, pp * n_pg + g]
            out.append(pltpu.make_async_copy(ck_hbm.at[page], kbuf.at[slot, g], sem.at[0, slot]))
            out.append(pltpu.make_async_copy(cv_hbm.at[page], vbuf.at[slot, g], sem.at[1, slot]))
        return out

    step = b * n_steps + p
    ahead = PAGE_SLOTS - 1

    @pl.when(step == 0)
    def _():
        for i in range(ahead):
            for c in page_copies(0, i, i):
                c.start()

    @pl.when(step + ahead < n_batch * n_steps)
    def _():
        pp = p + ahead
        wrap = (pp >= n_steps).astype(jnp.int32)
        for c in page_copies(b + wrap, pp - wrap * n_steps, lax.rem(step + ahead, PAGE_SLOTS)):
            c.start()

    slot = lax.rem(step, PAGE_SLOTS)
    for c in page_copies(b, p, slot):
        c.wait()

    def per_head_bias(bkt_ref):
        parts = []
        for h in range(N_HEADS):
            r0 = h * rows_per_head
            bk = bkt_ref[r0:r0 + rows_per_head, :]
            parts.append(jnp.where(bk >= 0, _bias_from_buckets(bk, rb_ref, h), NEG_BIG))
        return jnp.concatenate(parts, axis=0)

    @pl.when(jnp.logical_and(b == 0, p == 0))
    def _():
        mask_ref[...] = jnp.where(bkt_last_ref[...] >= 0, 0.0, NEG_BIG)
        bias_last_ref[...] = per_head_bias(bkt_last_ref)
        bias_new_ref[...] = per_head_bias(bkt_new_ref)
        knew[...] = jnp.zeros(knew.shape, BF16)
        v1new[:, 0:HEAD_W] = jnp.zeros((v1new.shape[0], HEAD_W), BF16)
        v1new[:, HEAD_W:2 * HEAD_W] = jnp.ones((v1new.shape[0], HEAD_W), BF16)
        v1all[:, HEAD_W:2 * HEAD_W] = jnp.ones((v1all.shape[0], HEAD_W), BF16)

    @pl.when(p == 0)
    def _():
        q = q_ref[...] * (DIFF_D ** -0.5 * LOG2E)
        lane = lax.broadcasted_iota(jnp.int32, (t_new, HEAD_W), 1)
        for h in range(N_HEADS):
            cols = slice(h * HEAD_W, (h + 1) * HEAD_W)
            r0 = h * rows_per_head
            qf[r0:r0 + t_new, :] = jnp.where(lane < DIFF_D, q[:, cols], 0.0)
            qf[r0 + t_new:r0 + 2 * t_new, :] = jnp.where(lane >= DIFF_D, q[:, cols], 0.0)
        qbf[...] = qf[...].astype(BF16)
        m_ref[...] = jnp.full(m_ref.shape, -jnp.inf, F32)
        acc_ref[...] = jnp.zeros(acc_ref.shape, F32)

    def add_mask(r0, j):
        return mask_ref[r0:r0 + SAMPLE_RB, j * HEAD_W:(j + 1) * HEAD_W]

    def add_last(r0, j):
        return bias_last_ref[r0:r0 + SAMPLE_RB, j * HEAD_W:(j + 1) * HEAD_W]

    def page_update(g, add_fn):
        pg = slice(g * pg_rows, (g + 1) * pg_rows)
        _softmax_tile(s_ref, p_ref, m_ref, alpha_ref.at[g], add_fn, rb=SAMPLE_RB, keep_in_regs=False,
                      cols=(g * pg_rows, pg_rows))
        _accumulate(acc_ref, alpha_ref.at[g], p_ref[:, pg], v1all[pg, :])

    def page_scores(g):
        pg = slice(g * pg_rows, (g + 1) * pg_rows)
        v1all[pg, 0:HEAD_W] = vbuf[slot, g].astype(BF16)
        s_ref[:, pg] = _dot_nt(qbf[...], kbuf[slot, g].astype(BF16))

    page_scores(0)
    for g in range(n_pg - 1):
        page_scores(g + 1)
        page_update(g, add_mask)

    @pl.when(p < last)
    def _():
        page_update(n_pg - 1, add_mask)

    @pl.when(p == last)
    def _():
        page_update(n_pg - 1, add_last)
        knew[0:n_new, :] = kn_ref[0].astype(BF16)
        v1new[0:n_new, 0:HEAD_W] = vn_ref[0].astype(BF16)
        sn_ref[...] = _dot_nt(qbf[...], knew[...])
        _softmax_tile(sn_ref, pn_ref, m_ref, alpha_ref.at[0],
                      lambda r0, j: bias_new_ref[r0:r0 + SAMPLE_RB, :], rb=SAMPLE_RB, keep_in_regs=True)
        _accumulate(acc_ref, alpha_ref.at[0], pn_ref[...], v1new[...])

        lam = _lam(lamp_ref, lam_init)
        for h in range(N_HEADS):
            cols = slice(h * HEAD_W, (h + 1) * HEAD_W)
            r0 = h * rows_per_head
            r1 = r0 + t_new
            o = (acc_ref[r0:r1, 0:HEAD_W] / acc_ref[r0:r1, HEAD_W:2 * HEAD_W]
                 - lam * (acc_ref[r1:r1 + t_new, 0:HEAD_W] / acc_ref[r1:r1 + t_new, HEAD_W:2 * HEAD_W]))
            o_ref[:, cols] = (_rms(o, gn_ref[:, cols]) * (1.0 - lam_init)).astype(o_ref.dtype)


def _diff_sample(proj_a, dk, dv, cache_k, cache_v, page_table, rel_bias, lamp, gn, *, t_new, lam_init, name):
    batch, n_pages = page_table.shape
    n_phys, page = cache_k.shape[0], cache_k.shape[1]
    past = n_pages * page
    rows = N_HEADS * 2 * t_new
    kv_rows = page * N_HEADS
    new_rows = 128
    assert t_new * N_HEADS <= new_rows
    q_head = np.repeat(np.arange(N_HEADS), 2 * t_new)
    q_pos = past + np.tile(np.arange(t_new), 2 * N_HEADS)

    def buckets(n_cols, first_key_pos, n_valid_cols):
        col = np.arange(n_cols)
        k_pos = first_key_pos + col // N_HEADS
        ok = ((col % N_HEADS)[None, :] == q_head[:, None]) & (k_pos[None, :] <= q_pos[:, None])
        ok = ok & (col < n_valid_cols)[None, :]
        return jnp.asarray(np.where(ok, _t5_bucket(q_pos, k_pos), -1).astype(np.int32))

    bkt_last = buckets(kv_rows, past - page, kv_rows)
    bkt_new = buckets(new_rows, past, t_new * N_HEADS)
    n_pg = math.gcd(n_pages, PAGES_PER_STEP)
    n_steps = n_pages // n_pg
    assert PAGE_SLOTS - 1 <= n_steps, "the first batch element must cover the ring's head start"
    grid_spec = pltpu.PrefetchScalarGridSpec(
        num_scalar_prefetch=1,
        grid=(batch, n_steps),
        in_specs=[pl.BlockSpec((t_new, GROUP_W), lambda b, p, pt: (b, N_SECTIONS_A - 1)),
                  pl.BlockSpec(memory_space=pl.ANY),
                  pl.BlockSpec(memory_space=pl.ANY),
                  pl.BlockSpec((1, t_new * N_HEADS, HEAD_W), lambda b, p, pt: (b, 0, 0)),
                  pl.BlockSpec((1, t_new * N_HEADS, HEAD_W), lambda b, p, pt: (b, 0, 0)),
                  pl.BlockSpec((rows, kv_rows), lambda b, p, pt: (0, 0)),
                  pl.BlockSpec((rows, new_rows), lambda b, p, pt: (0, 0)),
                  pl.BlockSpec(memory_space=pltpu.SMEM),
                  pl.BlockSpec((4, DIFF_D), lambda b, p, pt: (0, 0)),
                  pl.BlockSpec((1, GROUP_W), lambda b, p, pt: (0, 0))],
        out_specs=pl.BlockSpec((t_new, GROUP_W), lambda b, p, pt: (b, 0)),
        scratch_shapes=[pltpu.VMEM((rows, HEAD_W), F32), pltpu.VMEM((rows, HEAD_W), BF16),
                        pltpu.VMEM((n_pg * kv_rows, 2 * HEAD_W), BF16),
                        pltpu.VMEM((new_rows, HEAD_W), BF16), pltpu.VMEM((new_rows, 2 * HEAD_W), BF16),
                        pltpu.VMEM((rows, n_pg * kv_rows), F32), pltpu.VMEM((rows, n_pg * kv_rows), BF16),
                        pltpu.VMEM((rows, new_rows), F32), pltpu.VMEM((rows, new_rows), BF16),
                        pltpu.VMEM((rows, HEAD_W), F32), pltpu.VMEM((n_pg, rows, HEAD_W), F32),
                        pltpu.VMEM((rows, 2 * HEAD_W), F32),
                        pltpu.VMEM((rows, kv_rows), F32), pltpu.VMEM((rows, kv_rows), F32),
                        pltpu.VMEM((rows, new_rows), F32),
                        pltpu.VMEM((PAGE_SLOTS, n_pg, kv_rows, HEAD_W), F32),
                        pltpu.VMEM((PAGE_SLOTS, n_pg, kv_rows, HEAD_W), F32),
                        pltpu.SemaphoreType.DMA((2, PAGE_SLOTS))],
    )
    return pl.pallas_call(
        functools.partial(_diff_sample_kernel, t_new=t_new, lam_init=lam_init, n_pg=n_pg, n_steps=n_steps,
                          n_batch=batch),
        grid_spec=grid_spec,
        out_shape=jax.ShapeDtypeStruct((batch * t_new, GROUP_W), F32),
        compiler_params=_cparams(("arbitrary", "arbitrary")),
        name=name,
    )(page_table, proj_a, cache_k.reshape(n_phys, kv_rows, HEAD_W), cache_v.reshape(n_phys, kv_rows, HEAD_W),
      dk.reshape(batch, t_new * N_HEADS, HEAD_W), dv.reshape(batch, t_new * N_HEADS, HEAD_W),
      bkt_last, bkt_new, rel_bias, lamp, gn)


def _mix_out_kernel(ret_ref, diff_ref, w_ret_ref, w_diff_ref, gain_ref, x_ref, gain_q_ref, wq_ref, *refs, has_mem):
    if has_mem:
        mk_ref, mv_ref, x1_ref, o_ref = refs
    else:
        x1_ref, o_ref = refs
    y = _dot(ret_ref[...].astype(BF16), w_ret_ref[...]) + _dot(diff_ref[...].astype(BF16), w_diff_ref[...])
    x1 = x_ref[...] + _rms(y, gain_ref[...])
    x1_ref[...] = x1
    q = _dot(_rms(x1, gain_q_ref[...]).astype(BF16), wq_ref[...])
    if not has_mem:
        o_ref[...] = q.astype(o_ref.dtype)
        return
    for h in range(N_XHEADS):
        cols = slice(h * HEAD_W, (h + 1) * HEAD_W)
        s = _dot_nt(q[:, cols].astype(BF16), mk_ref[0, :, cols].astype(BF16)) * (HEAD_W ** -0.5)
        e = jnp.exp(s - jnp.max(s, axis=-1, keepdims=True))
        p = e / jnp.sum(e, axis=-1, keepdims=True)
        o_ref[:, cols] = _dot(p.astype(BF16), mv_ref[0, :, cols].astype(BF16)).astype(o_ref.dtype)


def _mix_out(ret_o, diff_o, w_out, gain, x, gain_q, w_xq, mem, *, tm, rows_per_batch, out_dtype, name):
    m, n = x.shape
    ka = ret_o.shape[1]
    xw = w_xq.shape[1]
    in_specs = [pl.BlockSpec((tm, ka), lambda i: (i, 0)), pl.BlockSpec((tm, ka), lambda i: (i, 0)),
                pl.BlockSpec((ka, n), lambda i: (0, 0)), pl.BlockSpec((ka, n), lambda i: (1, 0)),
                pl.BlockSpec((1, n), lambda i: (0, 0)), pl.BlockSpec((tm, n), lambda i: (i, 0)),
                pl.BlockSpec((1, n), lambda i: (0, 0)), pl.BlockSpec((n, xw), lambda i: (0, 0))]
    args = [ret_o, diff_o, w_out, w_out, gain, x, gain_q, w_xq]
    if mem is not None:
        assert rows_per_batch % tm == 0
        per = rows_per_batch // tm
        n_mem = mem[0].shape[1]
        in_specs += [pl.BlockSpec((1, n_mem, xw), lambda i: (i // per, 0, 0))] * 2
        args += list(mem)
    return pl.pallas_call(
        functools.partial(_mix_out_kernel, has_mem=mem is not None),
        grid=(m // tm,),
        in_specs=in_specs,
        out_specs=[pl.BlockSpec((tm, n), lambda i: (i, 0)), pl.BlockSpec((tm, xw), lambda i: (i, 0))],
        out_shape=[jax.ShapeDtypeStruct((m, n), F32), jax.ShapeDtypeStruct((m, xw), out_dtype)],
        compiler_params=_cparams(("parallel",)),
        name=name,
    )(*args)


def _cross_interleaved_kernel(q_ref, mk_ref, mv_ref, o_ref):
    tq = q_ref.shape[1]
    for e_i in range(q_ref.shape[0]):
        q = q_ref[e_i]
        qs = jnp.concatenate([q[:, h * HEAD_W:(h + 1) * HEAD_W] for h in range(N_XHEADS)], axis=0).astype(BF16)
        s = _dot_nt(qs, mk_ref[e_i].astype(BF16)) * (HEAD_W ** -0.5)
        row_head = lax.broadcasted_iota(jnp.int32, s.shape, 0) // tq
        col_head = lax.broadcasted_iota(jnp.int32, s.shape, 1) % N_XHEADS
        s = jnp.where(row_head == col_head, s, NEG_BIG)
        e = jnp.exp(s - jnp.max(s, axis=-1, keepdims=True))
        p = e / jnp.sum(e, axis=-1, keepdims=True)
        o = _dot(p.astype(BF16), mv_ref[e_i].astype(BF16))
        for h in range(N_XHEADS):
            o_ref[e_i, :, h * HEAD_W:(h + 1) * HEAD_W] = o[h * tq:(h + 1) * tq, :].astype(o_ref.dtype)


def _cross_attend_interleaved(q, mk, mv, *, name):
    b, t, w = q.shape
    rows = mk.shape[1]
    nb = math.gcd(b, CROSS_BATCH)
    return pl.pallas_call(
        _cross_interleaved_kernel,
        grid=(b // nb,),
        in_specs=[pl.BlockSpec((nb, t, w), lambda i: (i, 0, 0)),
                  pl.BlockSpec((nb, rows, HEAD_W), lambda i: (i, 0, 0)),
                  pl.BlockSpec((nb, rows, HEAD_W), lambda i: (i, 0, 0))],
        out_specs=pl.BlockSpec((nb, t, w), lambda i: (i, 0, 0)),
        out_shape=jax.ShapeDtypeStruct((b, t, w), q.dtype),
        compiler_params=_cparams(("parallel",)),
        name=name,
    )(q, mk, mv)


def _xo_ffn_kernel(x_ref, o_ref, wxo_ref, gx_ref, gpre_ref, wg_ref, wu_ref, wd_ref, gpost_ref, y_ref, *rest):
    x2_ref, h_ref, acc_ref = rest[-3:]
    copies = rest[:-3]
    f = pl.program_id(1)

    def weight(w_ref, i):
        w = w_ref[...]
        if copies:
            w = w.astype(BF16)
            copies[i][...] = w
        return w

    @pl.when(f == 0)
    def _():
        x2 = x_ref[...] + _rms(_dot(o_ref[...].astype(BF16), weight(wxo_ref, 0)), gx_ref[...])
        x2_ref[...] = x2
        h_ref[...] = _rms(x2, gpre_ref[...]).astype(BF16)
        acc_ref[...] = jnp.zeros(acc_ref.shape, F32)

    h = h_ref[...]
    g = _dot(h, weight(wg_ref, 1))
    a = (g * jax.nn.sigmoid(g)) * _dot(h, weight(wu_ref, 2))
    acc_ref[...] += _dot(a.astype(BF16), weight(wd_ref, 3))

    @pl.when(f == pl.num_programs(1) - 1)
    def _():
        y_ref[...] = x2_ref[...] + _rms(acc_ref[...], gpost_ref[...])


def _xo_ffn(x, o, w_xo, gx, gpre, wg, wu, wd, gpost, *, tm, tf, name):
    m, d = x.shape
    xw = o.shape[1]
    dff = wg.shape[1]
    emit_w = wg.dtype != BF16
    assert not emit_w or m == tm, "each weight tile must be visited once to be copied out"
    w_specs = [pl.BlockSpec((xw, d), lambda i, f: (0, 0)),
               pl.BlockSpec((d, tf), lambda i, f: (0, f)),
               pl.BlockSpec((d, tf), lambda i, f: (0, f)),
               pl.BlockSpec((tf, d), lambda i, f: (f, 0))]
    out_specs = [pl.BlockSpec((tm, d), lambda i, f: (i, 0))]
    out_shape = [jax.ShapeDtypeStruct((m, d), F32)]
    if emit_w:
        out_specs += w_specs
        out_shape += [jax.ShapeDtypeStruct(a.shape, BF16) for a in (w_xo, wg, wu, wd)]
    out = pl.pallas_call(
        _xo_ffn_kernel,
        grid=(m // tm, dff // tf),
        in_specs=[pl.BlockSpec((tm, d), lambda i, f: (i, 0)),
                  pl.BlockSpec((tm, xw), lambda i, f: (i, 0)),
                  w_specs[0],
                  pl.BlockSpec((1, d), lambda i, f: (0, 0)),
                  pl.BlockSpec((1, d), lambda i, f: (0, 0)),
                  w_specs[1], w_specs[2], w_specs[3],
                  pl.BlockSpec((1, d), lambda i, f: (0, 0))],
        out_specs=out_specs,
        out_shape=out_shape,
        scratch_shapes=[pltpu.VMEM((tm, d), F32), pltpu.VMEM((tm, d), BF16), pltpu.VMEM((tm, d), F32)],
        compiler_params=_cparams(("parallel", "arbitrary")),
        name=name,
    )(x, o, w_xo, gx, gpre, wg, wu, wd, gpost)
    return out if emit_w else out[0]


def _rope_tables(pos):
    inv_freq = ROPE_BASE ** (-jnp.arange(0, HEAD_W, 2, dtype=F32) / HEAD_W)
    ang = pos.astype(F32)[:, None] * inv_freq[None, :]
    cos = jnp.repeat(jnp.cos(ang), 2, axis=-1)
    sin = jnp.sin(ang)
    return cos, jnp.stack([-sin, sin], axis=-1).reshape(cos.shape)


def _layer(x, pos, s0, mem_k, mem_v, paged, w, *, batch, seq, lam_init, tag):
    m = batch * seq
    big = paged is None
    tm = 512 if big else m
    w = dict(w)
    proj_a, dk, dv, *w_in_copy = _in_proj(x, w["n_pre_mix"], w["w_in"], tm=1024 if big else m, tn=512,
                                          name=f"in_proj_{tag}")
    if w_in_copy:
        w["w_in"] = w_in_copy[0]
    cos, sin = _rope_tables(pos)
    ret_o, ret_s = _retention(proj_a, cos, sin, w["lg"], w["ret_gn"], s0, batch=batch, seq=seq,
                              heads=1 if big else N_HEADS, out_dtype=BF16 if big else F32, name=f"retention_{tag}")
    if big:
        diff_o = _diff_prompt(proj_a, dk, dv, w["rel_bias"], w["lamp"], w["diff_gn"], batch=batch, seq=seq,
                              t=512, lam_init=lam_init, name=f"diff_attn_{tag}")
    else:
        diff_o = _diff_sample(proj_a, dk, dv, *paged, w["rel_bias"], w["lamp"], w["diff_gn"], t_new=seq,
                              lam_init=lam_init, name=f"diff_attn_{tag}")
    xw = N_XHEADS * HEAD_W
    x, o = _mix_out(ret_o, diff_o, w["w_out"], w["n_post_mix"], x, w["n_pre_x"], w["w_xq"],
                    (mem_k, mem_v) if big else None, tm=tm, rows_per_batch=seq,
                    out_dtype=BF16 if big else F32, name=f"mix_out_{tag}")
    if not big:
        o = _cross_attend_interleaved(o.reshape(batch, seq, xw), mem_k, mem_v, name=f"cross_{tag}").reshape(m, xw)
    out = _xo_ffn(x, o, w["w_xo"], w["n_post_x"], w["n_pre_ffn"], w["w_gate"], w["w_up"], w["w_down"],
                  w["n_post_ffn"], tm=tm, tf=512 if big else 256, name=f"ffn_{tag}")
    if isinstance(out, (list, tuple)):
        x, w["w_xo"], w["w_gate"], w["w_up"], w["w_down"] = out
    else:
        x = out
    return x, dk, dv, ret_s, w


def kernel(x_prompt, x_sample, cache_k, cache_v, state_ret, cache_mem_k, cache_mem_v, page_table, mem_prompt,
           rel_bias, norm_pre_mix, norm_post_mix, norm_pre_x, norm_post_x, norm_pre_ffn, norm_post_ffn, norm_mem,
           w_in, w_out, ret_gn, diff_gn, lam_q1, lam_k1, lam_q2, lam_k2,
           w_xq, w_xk, w_xv, w_xo, w_gate, w_up, w_down):
    b_p, t_p, d = x_prompt.shape
    b_s, t_s, _ = x_sample.shape
    depth = w_in.shape[0]
    n_mem = mem_prompt.shape[1]
    xw = N_XHEADS * HEAD_W
    past_len = page_table.shape[1] * cache_k.shape[2]
    pos_p = jnp.arange(t_p)
    pos_s = past_len + jnp.arange(t_s)
    log_g = jnp.log1p(-(2.0 ** (-5.0 - jnp.arange(N_HEADS, dtype=F32))))
    lg = jnp.broadcast_to(log_g[:, None, None], (N_HEADS, 8, HEAD_W))
    xp = x_prompt.reshape(b_p * t_p, d)
    xs = x_sample.reshape(b_s * t_s, d)
    outs = [[] for _ in range(8)]
    for l in range(depth):
        lam_init = 0.8 - 0.6 * math.exp(-0.3 * l)
        w = {
            "n_pre_mix": norm_pre_mix[l][None], "n_post_mix": norm_post_mix[l][None],
            "n_pre_x": norm_pre_x[l][None], "n_post_x": norm_post_x[l][None],
            "n_pre_ffn": norm_pre_ffn[l][None], "n_post_ffn": norm_post_ffn[l][None],
            "w_in": w_in[l], "w_out": w_out[l].astype(BF16),
            "ret_gn": ret_gn[l][None], "diff_gn": diff_gn[l][None],
            "lamp": jnp.stack([lam_q1[l], lam_k1[l], lam_q2[l], lam_k2[l]]),
            "w_xq": w_xq[l].astype(BF16), "w_xo": w_xo[l],
            "w_gate": w_gate[l], "w_up": w_up[l], "w_down": w_down[l],
            "rel_bias": rel_bias, "lg": lg,
        }
        mem_flat = mem_prompt.reshape(b_p * n_mem, d)
        mk_p, mv_p = _norm_matmul_pair(mem_flat, norm_mem[l][None], w_xk[l], w_xv[l], name="mem_kv")
        s0 = jnp.zeros((b_p, N_HEADS, HEAD_W, HEAD_W), F32)
        xs, dk_s, dv_s, s_s, w = _layer(xs, pos_s, state_ret[l],
                                        cache_mem_k[l].reshape(b_s, n_mem * N_XHEADS, HEAD_W),
                                        cache_mem_v[l].reshape(b_s, n_mem * N_XHEADS, HEAD_W),
                                        (cache_k[l], cache_v[l], page_table), w,
                                        batch=b_s, seq=t_s, lam_init=lam_init, tag="sample")
        xp, dk_p, dv_p, s_p, _ = _layer(xp, pos_p, s0, mk_p.reshape(b_p, n_mem, xw), mv_p.reshape(b_p, n_mem, xw),
                                        None, w, batch=b_p, seq=t_p, lam_init=lam_init, tag="prompt")
        per_layer = (dk_p.reshape(b_p, t_p, N_HEADS, HEAD_W), dv_p.reshape(b_p, t_p, N_HEADS, HEAD_W), s_p,
                     mk_p.reshape(b_p, n_mem, N_XHEADS, HEAD_W), mv_p.reshape(b_p, n_mem, N_XHEADS, HEAD_W),
                     dk_s.reshape(b_s, t_s, N_HEADS, HEAD_W), dv_s.reshape(b_s, t_s, N_HEADS, HEAD_W), s_s)
        for acc, val in zip(outs, per_layer):
            acc.append(val)
    stacked = [jnp.stack(o) for o in outs]
    return (xp.reshape(b_p, t_p, d), xs.reshape(b_s, t_s, d), *stacked)
```
